```python
import math
import jax
import jax.numpy as jnp
from jax import lax
import numpy as np

D_MODEL = 2048
BATCH = 4
SEQ = 2048
DEPTH = 4
DEC_BATCH = 8
DEC_SEQ = 4
PAST_LEN = 16384
PAGE_SIZE = 128

N_HEADS_A = 8
HEAD_DIM_A = 64
V_DIM_A = 2 * HEAD_DIM_A
ROT_DIM = HEAD_DIM_A // 4
ROPE_THETA = 500000.0
WIDTH_A = N_HEADS_A * V_DIM_A
QK_WIDTH_A = N_HEADS_A * 2 * HEAD_DIM_A
Q_BLOCK = 128
N_HEADS_D = 8
HEAD_DIM_DK = 128
HEAD_DIM_DV = 128
WIDTH_DK = N_HEADS_D * HEAD_DIM_DK
WIDTH_DV = N_HEADS_D * HEAD_DIM_DV
CONV_W = 4
CONV_CH = 2 * WIDTH_DK + WIDTH_DV
CHUNK = 64
D_FF = 4 * D_MODEL
DEEPNORM_ALPHA = (2 * DEPTH) ** 0.25
DEEPNORM_BETA = (8 * DEPTH) ** -0.25
LN_EPS = 1e-5
RMS_EPS = 1e-6
IN_SPLITS = (QK_WIDTH_A, QK_WIDTH_A, WIDTH_A, CONV_CH, WIDTH_DV, N_HEADS_D, N_HEADS_D, D_MODEL, D_MODEL)
IN_COLS = sum(IN_SPLITS)

kernel_name = 'hybrid_diffattn_gdn_step'


def _split_cols(h):
    idx = []
    off = 0
    for s in IN_SPLITS[:-1]:
        off += s
        idx.append(off)
    return jnp.split(h, idx, axis=-1)


def _layernorm(x, g, b):
    xf = x.astype(jnp.float32)
    mu = jnp.mean(xf, axis=-1, keepdims=True)
    var = jnp.mean(jnp.square(xf - mu), axis=-1, keepdims=True)
    return ((xf - mu) * lax.rsqrt(var + LN_EPS) * g + b).astype(x.dtype)


def _rmsnorm(x, w):
    xf = x.astype(jnp.float32)
    return xf * lax.rsqrt(jnp.mean(xf * xf, axis=-1, keepdims=True) + RMS_EPS) * w


def _l2norm(t):
    return t * lax.rsqrt(jnp.sum(t * t, axis=-1, keepdims=True) + 1e-6)


def _rope_partial(x, pos):
    half = ROT_DIM // 2
    inv_freq = ROPE_THETA ** (-jnp.arange(half, dtype=jnp.float32) * 2.0 / ROT_DIM)
    ang = pos.astype(jnp.float32)[:, None] * inv_freq[None, :]
    cos = jnp.cos(ang)[:, None, None, :]
    sin = jnp.sin(ang)[:, None, None, :]
    xr = x[..., :ROT_DIM].astype(jnp.float32)
    x1, x2 = xr[..., :half], xr[..., half:]
    rot = jnp.concatenate([x1 * cos - x2 * sin, x2 * cos + x1 * sin], axis=-1)
    return jnp.concatenate([rot.astype(x.dtype), x[..., ROT_DIM:]], axis=-1)


def _diff_attention(q, k, v, q_pos, k_pos, lam):
    B, Sq = q.shape[0], q.shape[1]
    blk = Q_BLOCK if Sq % Q_BLOCK == 0 else Sq
    nb = Sq // blk
    scale = HEAD_DIM_A ** -0.5
    qb = q.reshape(B, nb, blk, N_HEADS_A, 2, HEAD_DIM_A).swapaxes(0, 1)
    pb = q_pos.reshape(nb, blk)
    v32 = v.astype(jnp.float32)

    def one_block(args):
        qi, pi = args
        s = jnp.einsum('bqhmd,bkhmd->bhmqk', qi, k, preferred_element_type=jnp.float32) * scale
        mask = k_pos[None, :] <= pi[:, None]
        p = jax.nn.softmax(jnp.where(mask, s, -jnp.inf), axis=-1)
        w = p[:, :, 0] - lam * p[:, :, 1]
        return jnp.einsum('bhqk,bkhd->bqhd', w, v32)

    o = lax.map(one_block, (qb, pb))
    return o.swapaxes(0, 1).reshape(B, Sq, N_HEADS_A, V_DIM_A)


def _short_conv(u, buf, w):
    S = u.shape[1]
    up = jnp.concatenate([buf.astype(u.dtype), u], axis=1)
    out = up[:, 0:S] * w[0]
    for i in range(1, CONV_W):
        out = out + up[:, i:i + S] * w[i]
    return out, up[:, S:]


def _gated_delta_chunked(q, k, v, g, beta, s0):
    B, S, H, DK = q.shape
    DV = v.shape[-1]
    C = CHUNK if S % CHUNK == 0 else S
    n = S // C

    def to_chunks(t):
        t = t.reshape(B, n, C, H, *t.shape[3:])
        return jnp.moveaxis(t, (1, 3), (0, 2))

    qc, kc, vc, gc, bc = (to_chunks(t) for t in (q, k, v, g, beta))
    G = jnp.cumsum(gc, axis=-1)
    idx = jnp.arange(C)
    causal = idx[:, None] >= idx[None, :]
    strict = idx[:, None] > idx[None, :]
    decay = jnp.exp(jnp.where(causal, G[..., :, None] - G[..., None, :], -jnp.inf))
    kb = kc * bc[..., None]
    vb = vc * bc[..., None]
    L = jnp.einsum('nbhid,nbhjd->nbhij', kb, kc) * jnp.where(strict, decay, 0.0)
    A = L + jnp.eye(C, dtype=jnp.float32)
    rhs = jnp.concatenate([vb, kb * jnp.exp(G)[..., None]], axis=-1)
    sol = lax.linalg.triangular_solve(A, rhs, left_side=True, lower=True, unit_diagonal=True)
    val, kcd = sol[..., :DV], sol[..., DV:]
    qk = jnp.einsum('nbhid,nbhjd->nbhij', qc, kc) * decay
    q_dec = qc * jnp.exp(G)[..., None]
    k_dec = kc * jnp.exp(G[..., -1:] - G)[..., None]
    g_last = jnp.exp(G[..., -1])

    def step(st, xs):
        val_i, kcd_i, qk_i, qd_i, kd_i, gl_i = xs
        v_new = val_i - jnp.einsum('bhck,bhkv->bhcv', kcd_i, st)
        o = jnp.einsum('bhck,bhkv->bhcv', qd_i, st) + jnp.einsum('bhij,bhjv->bhiv', qk_i, v_new)
        st = st * gl_i[..., None, None] + jnp.einsum('bhck,bhcv->bhkv', kd_i, v_new)
        return st, o

    s_fin, o = lax.scan(step, s0, (val, kcd, qk, q_dec, k_dec, g_last))
    o = jnp.moveaxis(o, (0, 2), (1, 3)).reshape(B, S, H, DV)
    return o, s_fin


def _layer(x, pos, lidx, past_k, past_v, ssm0, conv0, p):
    (w_in, conv_w, a_log, dt_bias, dnorm_w, lq1, lk1, lq2, lk2, anorm_w,
     w_ba, w_bb, w_out, ln1_g, ln1_b, w_up, w_down, ln2_g, ln2_b) = p
    f32 = jnp.float32
    B, S, _ = x.shape
    h = x @ w_in
    qa, ka, va, qkv_d, z_d, beta_raw, a_raw, gate_a, gate_b = _split_cols(h)

    qa = _rope_partial(qa.reshape(B, S, N_HEADS_A, 2, HEAD_DIM_A), pos)
    ka = _rope_partial(ka.reshape(B, S, N_HEADS_A, 2, HEAD_DIM_A), pos)
    va = va.reshape(B, S, N_HEADS_A, V_DIM_A)
    if past_k is None:
        k_all, v_all, k_pos = ka, va, pos
    else:
        k_all = jnp.concatenate([past_k.astype(ka.dtype), ka], axis=1)
        v_all = jnp.concatenate([past_v.astype(va.dtype), va], axis=1)
        k_pos = jnp.arange(past_k.shape[1] + S)
    lam_init = 0.8 - 0.6 * math.exp(-0.3 * lidx)
    lam = (jnp.exp(jnp.sum(lq1.astype(f32) * lk1.astype(f32)))
           - jnp.exp(jnp.sum(lq2.astype(f32) * lk2.astype(f32))) + lam_init)
    o_a = _diff_attention(qa, k_all, v_all, pos, k_pos, lam)
    o_a = (_rmsnorm(o_a, anorm_w) * (1.0 - lam_init)).reshape(B, S, WIDTH_A).astype(x.dtype)

    u, conv_new = _short_conv(qkv_d, conv0, conv_w)
    u = jax.nn.silu(u.astype(f32))
    q_d, k_d, v_d = jnp.split(u, [WIDTH_DK, 2 * WIDTH_DK], axis=-1)
    q_d = _l2norm(q_d.reshape(B, S, N_HEADS_D, HEAD_DIM_DK)) * (HEAD_DIM_DK ** -0.5)
    k_d = _l2norm(k_d.reshape(B, S, N_HEADS_D, HEAD_DIM_DK))
    v_d = v_d.reshape(B, S, N_HEADS_D, HEAD_DIM_DV)
    beta = jax.nn.sigmoid(beta_raw.astype(f32))
    g = -jnp.exp(a_log.astype(f32)) * jax.nn.softplus(a_raw.astype(f32) + dt_bias.astype(f32))
    o_d, ssm_new = _gated_delta_chunked(q_d, k_d, v_d, g, beta, ssm0.astype(f32))
    o_d = _rmsnorm(o_d, dnorm_w) * jax.nn.silu(z_d.astype(f32).reshape(B, S, N_HEADS_D, HEAD_DIM_DV))
    o_d = o_d.reshape(B, S, WIDTH_DV).astype(x.dtype)

    mixed = jax.nn.sigmoid(gate_a) * (o_a @ w_ba) + jax.nn.sigmoid(gate_b) * (o_d @ w_bb)
    x = _layernorm(DEEPNORM_ALPHA * x + mixed @ w_out, ln1_g, ln1_b)

    hid = jnp.square(jax.nn.relu(x @ w_up))
    x = _layernorm(DEEPNORM_ALPHA * x + hid @ w_down, ln2_g, ln2_b)
    return x, ka, va, ssm_new, conv_new


def setup_inputs(seed: int = 0) -> dict:
    key = jax.random.key(seed)
    ks = jax.random.split(key, 32)
    f32 = jnp.float32
    n_pages = PAST_LEN // PAGE_SIZE
    n_used = DEC_BATCH * n_pages
    n_pool = n_used + max(1, n_used // 4)

    def nrm(k, shape, s):
        return jax.random.normal(k, shape, f32) * s

    x_prompt = nrm(ks[0], (BATCH, SEQ, D_MODEL), 1.0)
    x_sample = nrm(ks[1], (DEC_BATCH, DEC_SEQ, D_MODEL), 1.0)
    cache_k = nrm(ks[2], (DEPTH, n_pool, PAGE_SIZE, N_HEADS_A, 2, HEAD_DIM_A), 1.0)
    cache_v = nrm(ks[3], (DEPTH, n_pool, PAGE_SIZE, N_HEADS_A, V_DIM_A), 1.0)
    state_ssm = nrm(ks[4], (DEPTH, DEC_BATCH, N_HEADS_D, HEAD_DIM_DK, HEAD_DIM_DV), 0.1)
    state_conv = nrm(ks[5], (DEPTH, DEC_BATCH, CONV_W - 1, CONV_CH), 1.0)
    perm = jax.random.permutation(ks[6], n_pool)
    page_table = perm[:n_used].reshape(DEC_BATCH, n_pages).astype(jnp.int32)

    w_in = nrm(ks[7], (DEPTH, D_MODEL, IN_COLS), D_MODEL ** -0.5)
    conv_w = nrm(ks[8], (DEPTH, CONV_W, CONV_CH), CONV_W ** -0.5)
    a_log = jnp.log(jax.random.uniform(ks[9], (DEPTH, N_HEADS_D), f32, 1.0, 16.0))
    dt = jnp.exp(jax.random.uniform(ks[10], (DEPTH, N_HEADS_D), f32, math.log(1e-3), math.log(1e-1)))
    dt_bias = dt + jnp.log(-jnp.expm1(-dt))
    delta_norm_w = 1.0 + nrm(ks[11], (DEPTH, HEAD_DIM_DV), 0.02)
    lambda_q1 = nrm(ks[12], (DEPTH, HEAD_DIM_A), 0.1)
    lambda_k1 = nrm(ks[13], (DEPTH, HEAD_DIM_A), 0.1)
    lambda_q2 = nrm(ks[14], (DEPTH, HEAD_DIM_A), 0.1)
    lambda_k2 = nrm(ks[15], (DEPTH, HEAD_DIM_A), 0.1)
    attn_norm_w = 1.0 + nrm(ks[16], (DEPTH, V_DIM_A), 0.02)
    w_branch_a = nrm(ks[17], (DEPTH, WIDTH_A, D_MODEL), WIDTH_A ** -0.5)
    w_branch_b = nrm(ks[18], (DEPTH, WIDTH_DV, D_MODEL), WIDTH_DV ** -0.5)
    w_out = nrm(ks[19], (DEPTH, D_MODEL, D_MODEL), D_MODEL ** -0.5 * DEEPNORM_BETA)
    ln1_g = 1.0 + nrm(ks[20], (DEPTH, D_MODEL), 0.02)
    ln1_b = nrm(ks[21], (DEPTH, D_MODEL), 0.02)
    w_mlp_up = nrm(ks[22], (DEPTH, D_MODEL, D_FF), D_MODEL ** -0.5)
    w_mlp_down = nrm(ks[23], (DEPTH, D_FF, D_MODEL), D_FF ** -0.5 * DEEPNORM_BETA)
    ln2_g = 1.0 + nrm(ks[24], (DEPTH, D_MODEL), 0.02)
    ln2_b = nrm(ks[25], (DEPTH, D_MODEL), 0.02)
    return {'x_prompt': x_prompt, 'x_sample': x_sample, 'cache_k': cache_k, 'cache_v': cache_v,
            'state_ssm': state_ssm, 'state_conv': state_conv, 'page_table': page_table,
            'w_in': w_in, 'conv_w': conv_w, 'a_log': a_log, 'dt_bias': dt_bias,
            'delta_norm_w': delta_norm_w, 'lambda_q1': lambda_q1, 'lambda_k1': lambda_k1,
            'lambda_q2': lambda_q2, 'lambda_k2': lambda_k2, 'attn_norm_w': attn_norm_w,
            'w_branch_a': w_branch_a, 'w_branch_b': w_branch_b, 'w_out': w_out,
            'ln1_g': ln1_g, 'ln1_b': ln1_b, 'w_mlp_up': w_mlp_up, 'w_mlp_down': w_mlp_down,
            'ln2_g': ln2_g, 'ln2_b': ln2_b}


def reference(x_prompt, x_sample, cache_k, cache_v, state_ssm, state_conv, page_table,
              w_in, conv_w, a_log, dt_bias, delta_norm_w, lambda_q1, lambda_k1, lambda_q2,
              lambda_k2, attn_norm_w, w_branch_a, w_branch_b, w_out, ln1_g, ln1_b,
              w_mlp_up, w_mlp_down, ln2_g, ln2_b):
    bp, sp, _ = x_prompt.shape
    bs, ss, _ = x_sample.shape
    n_pages = page_table.shape[1]
    past_len = n_pages * cache_k.shape[2]
    pos_p = jnp.arange(sp)
    pos_s = past_len + jnp.arange(ss)
    ssm0_p = jnp.zeros((bp, N_HEADS_D, HEAD_DIM_DK, HEAD_DIM_DV), jnp.float32)
    conv0_p = jnp.zeros((bp, CONV_W - 1, CONV_CH), x_prompt.dtype)
    xp, xs = x_prompt, x_sample
    kp_l, vp_l, sp_l, cp_l = [], [], [], []
    ks_l, vs_l, ss_l, cs_l = [], [], [], []
    for l in range(DEPTH):
        p = (w_in[l], conv_w[l], a_log[l], dt_bias[l], delta_norm_w[l], lambda_q1[l], lambda_k1[l],
             lambda_q2[l], lambda_k2[l], attn_norm_w[l], w_branch_a[l], w_branch_b[l], w_out[l],
             ln1_g[l], ln1_b[l], w_mlp_up[l], w_mlp_down[l], ln2_g[l], ln2_b[l])
        xp, k_, v_, s_, c_ = _layer(xp, pos_p, l, None, None, ssm0_p, conv0_p, p)
        kp_l.append(k_)
        vp_l.append(v_)
        sp_l.append(s_)
        cp_l.append(c_)
        past_k = cache_k[l, page_table].reshape(bs, past_len, N_HEADS_A, 2, HEAD_DIM_A)
        past_v = cache_v[l, page_table].reshape(bs, past_len, N_HEADS_A, V_DIM_A)
        xs, k_, v_, s_, c_ = _layer(xs, pos_s, l, past_k, past_v, state_ssm[l], state_conv[l], p)
        ks_l.append(k_)
        vs_l.append(v_)
        ss_l.append(s_)
        cs_l.append(c_)
    return (xp, xs, jnp.stack(kp_l), jnp.stack(vp_l), jnp.stack(sp_l), jnp.stack(cp_l),
            jnp.stack(ks_l), jnp.stack(vs_l), jnp.stack(ss_l), jnp.stack(cs_l))
```

```python
import functools
import math

import jax
import jax.numpy as jnp
from jax import lax
from jax.experimental import pallas as pl
from jax.experimental.pallas import tpu as pltpu

F32 = jnp.float32
BF16 = jnp.bfloat16
LANE = 128
SUBLANE = 8
MIB = 1024 * 1024

ROPE_THETA = 500000.0
LN_EPS = 1e-5
RMS_EPS = 1e-6
L2_EPS = 1e-6
GDN_CHUNK = 128
NEG_BIG = -1e30


def _cparams(sem, vmem_mib):
    return pltpu.CompilerParams(dimension_semantics=sem, vmem_limit_bytes=vmem_mib * MIB)


def _pick(n, cands):
    for c in cands:
        if n % c == 0:
            return c
    raise ValueError(f"no tile for {n} in {cands}")


def _log2(n):
    assert n & (n - 1) == 0
    return n.bit_length() - 1


def _sigmoid(x):
    return 1.0 / (1.0 + jnp.exp(-x))


def _bdot(a, b):
    return jnp.dot(a.astype(BF16), b.astype(BF16), preferred_element_type=F32)


def _bdot_nt(a, b):
    return lax.dot_general(a.astype(BF16), b.astype(BF16), (((1,), (1,)), ((), ())),
                           preferred_element_type=F32)


def _split2(a):
    hi = a.astype(BF16)
    lo = (a - hi.astype(F32)).astype(BF16)
    return hi, lo


def _dot3(a, b):
    ah, al = _split2(a)
    bh, bl = _split2(b)
    d = functools.partial(jnp.dot, preferred_element_type=F32)
    return d(ah, bh) + (d(ah, bl) + d(al, bh))


def _dot_exact_lhs(mask_bf16, b):
    b1 = b.astype(BF16)
    r1 = b - b1.astype(F32)
    b2 = r1.astype(BF16)
    b3 = (r1 - b2.astype(F32)).astype(BF16)
    d = functools.partial(jnp.dot, preferred_element_type=F32)
    return d(mask_bf16, b1) + (d(mask_bf16, b2) + d(mask_bf16, b3))


def _inproj_kernel(x_ref, w_ref, c_ref, sa_ref, sb_ref, o_ref, *, n_rope_blocks, tn, half):
    j = pl.program_id(1)
    acc = _bdot(x_ref[...], w_ref[...])

    @pl.when(j >= n_rope_blocks)
    def _():
        o_ref[...] = acc

    @pl.when(j < n_rope_blocks)
    def _():
        c, sa, sb = c_ref[...], sa_ref[...], sb_ref[...]
        for g in range(tn // LANE):
            xg = acc[:, g * LANE:(g + 1) * LANE]
            o_ref[:, g * LANE:(g + 1) * LANE] = (
                xg * c + pltpu.roll(xg, LANE - half, 1) * sa + pltpu.roll(xg, half, 1) * sb)


def _inproj(x, w_in, layer, tabs, *, n_main, n_rope, half, tm, tn):
    m, d = x.shape
    c, sa, sb = tabs
    tab_spec = pl.BlockSpec((tm, LANE), lambda i, j: (i, 0))
    return pl.pallas_call(
        functools.partial(_inproj_kernel, n_rope_blocks=n_rope // tn, tn=tn, half=half),
        grid=(m // tm, n_main // tn),
        in_specs=[pl.BlockSpec((tm, d), lambda i, j: (i, 0)),
                  pl.BlockSpec((None, d, tn), lambda i, j: (layer, 0, j)),
                  tab_spec, tab_spec, tab_spec],
        out_specs=pl.BlockSpec((tm, tn), lambda i, j: (i, j)),
        out_shape=jax.ShapeDtypeStruct((m, n_main), F32),
        compiler_params=_cparams(("parallel", "arbitrary"), 48),
        name="inproj",
    )(x, w_in, c, sa, sb)


def _gates_kernel(x_ref, w_ref, o_ref):
    o_ref[...] = _sigmoid(_bdot(x_ref[...], w_ref[...]))


def _gates(x, w_gates, layer, *, tm, tn):
    m, d = x.shape
    n = w_gates.shape[2]
    return pl.pallas_call(
        _gates_kernel,
        grid=(m // tm, n // tn),
        in_specs=[pl.BlockSpec((tm, d), lambda i, j: (i, 0)),
                  pl.BlockSpec((None, d, tn), lambda i, j: (layer, 0, j))],
        out_specs=pl.BlockSpec((tm, tn), lambda i, j: (i, j)),
        out_shape=jax.ShapeDtypeStruct((m, n), F32),
        compiler_params=_cparams(("parallel", "arbitrary"), 48),
        name="gates",
    )(x, w_gates)


def _small_kernel(x_ref, w_ref, alog_ref, dtb_ref, o_ref, *, n_heads):
    h = _bdot(x_ref[...], w_ref[...])
    lane = lax.broadcasted_iota(jnp.int32, h.shape, 1)
    beta = _sigmoid(h)
    t = h + dtb_ref[...]
    softplus = jnp.maximum(t, 0.0) + jnp.log1p(jnp.exp(-jnp.abs(t)))
    g = -jnp.exp(alog_ref[...]) * softplus
    o_ref[...] = jnp.where(lane < n_heads, beta, jnp.where(lane < 2 * n_heads, g, 0.0))


def _small(x, w_small, layer, alog, dtb, *, n_heads, tm):
    m, d = x.shape
    vec = pl.BlockSpec((None, 1, LANE), lambda i: (layer, 0, 0))
    return pl.pallas_call(
        functools.partial(_small_kernel, n_heads=n_heads),
        grid=(m // tm,),
        in_specs=[pl.BlockSpec((tm, d), lambda i: (i, 0)),
                  pl.BlockSpec((None, d, LANE), lambda i: (layer, 0, 0)),
                  vec, vec],
        out_specs=pl.BlockSpec((tm, LANE), lambda i: (i, 0)),
        out_shape=jax.ShapeDtypeStruct((m, LANE), F32),
        compiler_params=_cparams(("parallel",), 32),
        name="small",
    )(x, w_small, alog, dtb)


def _rms_head(o, nw, post_scale):
    return o * lax.rsqrt(jnp.mean(o * o, axis=-1, keepdims=True) + RMS_EPS) * nw * post_scale


def _attn_prompt_kernel(q_ref, k_ref, v_ref, lam_ref, nw_ref, o_ref, *, tq, tk, hd, scale, post_scale):
    qi = pl.program_id(2)
    q = q_ref[...] * scale
    lane = lax.broadcasted_iota(jnp.int32, q.shape, 1)
    q1 = jnp.where(lane < hd, q, 0.0).astype(BF16)
    q2 = jnp.where(lane >= hd, q, 0.0).astype(BF16)
    row = qi * tq + lax.broadcasted_iota(jnp.int32, (tq, tk), 0)
    col = lax.broadcasted_iota(jnp.int32, (tq, tk), 1)

    def update(s, v, m, l, a):
        mn = jnp.maximum(m, jnp.max(s, axis=-1, keepdims=True))
        p = jnp.exp(s - mn)
        al = jnp.exp(m - mn)
        l = al * l + jnp.sum(p, axis=-1, keepdims=True)
        a = al * a + jnp.dot(p.astype(BF16), v, preferred_element_type=F32)
        return mn, l, a

    def body(kb, carry):
        m1, l1, a1, m2, l2, a2 = carry
        start = pl.multiple_of(kb * tk, tk)
        k = k_ref[pl.ds(start, tk), :].astype(BF16)
        v = v_ref[pl.ds(start, tk), :].astype(BF16)
        mask = (start + col) <= row
        s1 = jnp.where(mask, _bdot_nt(q1, k), NEG_BIG)
        s2 = jnp.where(mask, _bdot_nt(q2, k), NEG_BIG)
        m1, l1, a1 = update(s1, v, m1, l1, a1)
        m2, l2, a2 = update(s2, v, m2, l2, a2)
        return m1, l1, a1, m2, l2, a2

    col1 = jnp.full((tq, 1), NEG_BIG, F32)
    z1 = jnp.zeros((tq, 1), F32)
    za = jnp.zeros((tq, LANE), F32)
    n_kb = ((qi + 1) * tq) // tk
    m1, l1, a1, m2, l2, a2 = lax.fori_loop(0, n_kb, body, (col1, z1, za, col1, z1, za))
    o = a1 / l1 - lam_ref[...] * (a2 / l2)
    o_ref[...] = _rms_head(o, nw_ref[...], post_scale).astype(o_ref.dtype)


def _attn_prompt(hmain, lamv, nw, layer, *, batch, seq, n_heads, hd, post_scale, tq, tk, out_dtype):
    m = hmain.shape[0]
    nq = seq // tq
    vec = pl.BlockSpec((None, 1, LANE), lambda b, h, i: (layer, 0, 0))
    return pl.pallas_call(
        functools.partial(_attn_prompt_kernel, tq=tq, tk=tk, hd=hd, scale=hd ** -0.5,
                          post_scale=post_scale),
        grid=(batch, n_heads, nq),
        in_specs=[pl.BlockSpec((tq, LANE), lambda b, h, i: (b * nq + i, h)),
                  pl.BlockSpec((seq, LANE), lambda b, h, i: (b, n_heads + h)),
                  pl.BlockSpec((seq, LANE), lambda b, h, i: (b, 2 * n_heads + h)),
                  vec, vec],
        out_specs=pl.BlockSpec((tq, LANE), lambda b, h, i: (b * nq + i, h)),
        out_shape=jax.ShapeDtypeStruct((m, n_heads * LANE), out_dtype),
        compiler_params=_cparams(("parallel", "parallel", "arbitrary"), 32),
        name="attn_prompt",
    )(hmain, hmain, hmain, lamv, nw)


def _attn_decode_kernel(pt_ref, q_ref, kn_ref, vn_ref, lam_ref, nw_ref, *rest,
                        pp, n_heads, hd, n_tok, scale, post_scale):
    k_refs, v_refs = rest[:pp], rest[pp:2 * pp]
    o_ref = rest[2 * pp]
    qbd_ref, m_ref, l_ref, acc_ref, kpad_ref, vpad_ref = rest[2 * pp + 1:]
    p_id = pl.program_id(1)
    rows = n_heads * SUBLANE
    width = n_heads * LANE

    def update(s_parts, v_parts):
        s = jnp.concatenate(s_parts, axis=1) if len(s_parts) > 1 else s_parts[0]
        m_old = m_ref[:, :1]
        mn = jnp.maximum(m_old, jnp.max(s, axis=-1, keepdims=True))
        p = jnp.exp(s - mn)
        al = jnp.exp(m_old - mn)
        l_ref[...] = jnp.broadcast_to(al * l_ref[:, :1] + jnp.sum(p, axis=-1, keepdims=True), (rows, LANE))
        m_ref[...] = jnp.broadcast_to(mn, (rows, LANE))
        pv = None
        for i, v in enumerate(v_parts):
            t = jnp.dot(p[:, i * LANE:(i + 1) * LANE].astype(BF16), v, preferred_element_type=F32)
            pv = t if pv is None else pv + t
        acc_ref[...] = al * acc_ref[...] + pv

    @pl.when(p_id == 0)
    def _():
        q = q_ref[...] * scale
        r8 = lax.broadcasted_iota(jnp.int32, q.shape, 0)
        q8 = jnp.where(r8 < n_tok, q, pltpu.roll(q, n_tok, 0))
        qt = jnp.concatenate([q8] * n_heads, axis=0)
        r = lax.broadcasted_iota(jnp.int32, qt.shape, 0)
        c = lax.broadcasted_iota(jnp.int32, qt.shape, 1)
        sel = (c >> _log2(hd)) == (r >> _log2(SUBLANE)) * 2 + ((r & (SUBLANE - 1)) >> _log2(n_tok))
        qbd_ref[...] = jnp.where(sel, qt, 0.0).astype(BF16)
        m_ref[...] = jnp.full(m_ref.shape, NEG_BIG, F32)
        l_ref[...] = jnp.zeros(l_ref.shape, F32)
        acc_ref[...] = jnp.zeros(acc_ref.shape, F32)
        kpad_ref[...] = jnp.zeros(kpad_ref.shape, F32)
        vpad_ref[...] = jnp.zeros(vpad_ref.shape, F32)
        kpad_ref[0:SUBLANE, :] = kn_ref[...]
        vpad_ref[0:SUBLANE, :] = vn_ref[...]
        s = _bdot_nt(qbd_ref[...], kpad_ref[...])
        rr = lax.broadcasted_iota(jnp.int32, s.shape, 0)
        cc = lax.broadcasted_iota(jnp.int32, s.shape, 1)
        s = jnp.where(cc <= (rr & (n_tok - 1)), s, NEG_BIG)
        update([s], [vpad_ref[...].astype(BF16)])

    qbd = qbd_ref[...]
    s_parts = [_bdot_nt(qbd, k_refs[i][...]) for i in range(pp)]
    v_parts = [v_refs[i][...].astype(BF16) for i in range(pp)]
    update(s_parts, v_parts)

    @pl.when(p_id == pl.num_programs(1) - 1)
    def _():
        accn = acc_ref[...] / l_ref[:, :1]
        c = lax.broadcasted_iota(jnp.int32, (SUBLANE, width), 1)
        res = jnp.zeros((SUBLANE, width), F32)
        for h in range(n_heads):
            res = res + jnp.where((c >> _log2(LANE)) == h, accn[h * SUBLANE:(h + 1) * SUBLANE, :], 0.0)
        o8 = res - lam_ref[...][:, :1] * pltpu.roll(res, SUBLANE - n_tok, 0)
        nw = nw_ref[...]
        for h in range(n_heads):
            o_ref[:, h * LANE:(h + 1) * LANE] = _rms_head(o8[:, h * LANE:(h + 1) * LANE], nw, post_scale)


def _attn_decode(hmain, cache_k, cache_v, page_table, lamv, nw, layer, *, n_heads, hd, n_tok,
                 post_scale, pp):
    m = hmain.shape[0]
    batch, n_pages = page_table.shape
    page = cache_k.shape[2]
    width = n_heads * LANE
    rows = n_heads * SUBLANE
    assert 2 * n_tok == SUBLANE and page == LANE and n_pages % pp == 0
    vec = pl.BlockSpec((None, 1, LANE), lambda b, p, pt: (layer, 0, 0))

    def page_spec(i):
        return pl.BlockSpec((None, None, page, width),
                            lambda b, p, pt: (layer, pt[b, p * pp + i], 0, 0))

    in_specs = ([pl.BlockSpec((SUBLANE, width), lambda b, p, pt: (b, 0)),
                 pl.BlockSpec((SUBLANE, width), lambda b, p, pt: (b, 1)),
                 pl.BlockSpec((SUBLANE, width), lambda b, p, pt: (b, 2)),
                 vec, vec]
                + [page_spec(i) for i in range(pp)] + [page_spec(i) for i in range(pp)])
    grid_spec = pltpu.PrefetchScalarGridSpec(
        num_scalar_prefetch=1,
        grid=(batch, n_pages // pp),
        in_specs=in_specs,
        out_specs=pl.BlockSpec((SUBLANE, width), lambda b, p, pt: (b, 0)),
        scratch_shapes=[pltpu.VMEM((rows, width), BF16),
                        pltpu.VMEM((rows, LANE), F32),
                        pltpu.VMEM((rows, LANE), F32),
                        pltpu.VMEM((rows, width), F32),
                        pltpu.VMEM((page, width), F32),
                        pltpu.VMEM((page, width), F32)])
    return pl.pallas_call(
        functools.partial(_attn_decode_kernel, pp=pp, n_heads=n_heads, hd=hd, n_tok=n_tok,
                          scale=hd ** -0.5, post_scale=post_scale),
        grid_spec=grid_spec,
        out_shape=jax.ShapeDtypeStruct((m, width), F32),
        compiler_params=_cparams(("parallel", "arbitrary"), 48),
        name="attn_decode",
    )(page_table, hmain, hmain, hmain, lamv, nw, *([cache_k] * pp), *([cache_v] * pp))


def _gdn_kernel(u_ref, z_ref, bg_ref, buf_ref, s0_ref, cw_ref, dn_ref, o_ref, s_out_ref,
                halo_ref, st_ref, *, cin, n_valid, n_heads, dk, dv, conv_w):
    c_id = pl.program_id(1)
    C = GDN_CHUNK
    width = n_heads * dk

    @pl.when(c_id == 0)
    def _():
        halo_ref[...] = buf_ref[...]
        st_ref[...] = s0_ref[...]

    u = u_ref[...]
    ext = jnp.concatenate([halo_ref[...], u], axis=0)
    cw = cw_ref[...]
    conv = None
    for i in range(conv_w):
        sh = SUBLANE - (conv_w - 1) + i
        term = ext[sh:sh + cin, :] * cw[i:i + 1, :]
        conv = term if conv is None else conv + term
    halo_ref[...] = u[cin - SUBLANE:cin, :]
    act = conv * _sigmoid(conv)
    bg = bg_ref[...]
    if n_valid < cin:
        valid = lax.broadcasted_iota(jnp.int32, (cin, 1), 0) < n_valid
        act = jnp.where(valid, act, 0.0)
        bg = jnp.where(valid, bg, 0.0)

    def pad_rows(a):
        if cin == C:
            return a
        return jnp.concatenate([a, jnp.zeros((C - cin, a.shape[1]), a.dtype)], axis=0)

    ri = lax.broadcasted_iota(jnp.int32, (C, C), 0)
    ci = lax.broadcasted_iota(jnp.int32, (C, C), 1)
    causal = ri >= ci
    strict = ri > ci
    tril = jnp.where(causal, 1.0, 0.0).astype(BF16)
    eye = jnp.where(ri == ci, 1.0, 0.0).astype(F32)
    n_levels = C.bit_length() - 1
    z = z_ref[...]
    dn = dn_ref[...]

    for h in range(n_heads):
        q = act[:, h * dk:(h + 1) * dk]
        k = act[:, width + h * dk:width + (h + 1) * dk]
        v = act[:, 2 * width + h * dv:2 * width + (h + 1) * dv]
        q = q * lax.rsqrt(jnp.sum(q * q, axis=-1, keepdims=True) + L2_EPS) * (dk ** -0.5)
        k = k * lax.rsqrt(jnp.sum(k * k, axis=-1, keepdims=True) + L2_EPS)
        beta = jnp.broadcast_to(bg[:, h:h + 1], (cin, dk))
        g = jnp.broadcast_to(bg[:, n_heads + h:n_heads + h + 1], (cin, C))
        q, k, v, beta, g = (pad_rows(t) for t in (q, k, v, beta, g))

        gc = _dot_exact_lhs(tril, g)
        gr = gc.T
        decay = jnp.exp(jnp.where(causal, gc - gr, NEG_BIG))
        kb = k * beta
        vb = v * beta
        low = _bdot_nt(kb, k) * jnp.where(strict, decay, 0.0)
        t_inv = eye
        for lvl in range(n_levels):
            inside = ((ri >> (lvl + 1)) == (ci >> (lvl + 1))) & ((ri >> lvl) != (ci >> lvl))
            e = jnp.where(inside, low, 0.0)
            t_inv = t_inv - _dot3(t_inv, _dot3(e, t_inv))
        eg = jnp.exp(gc)
        val = _dot3(t_inv, vb)
        kcd = _dot3(t_inv, kb * eg)
        qk = _bdot_nt(q, k) * decay
        g_last = gc[C - 1:C, :]
        q_dec = q * eg
        k_dec = k * jnp.exp(g_last - gc)
        st = st_ref[h]
        v_new = val - _bdot(kcd, st)
        o = _bdot(q_dec, st) + _bdot(qk, v_new)
        st_ref[h] = st * jnp.exp(g_last) + _bdot(k_dec.T, v_new)
        o = o[:cin, :]
        zh = z[:, h * dv:(h + 1) * dv]
        o = o * lax.rsqrt(jnp.mean(o * o, axis=-1, keepdims=True) + RMS_EPS) * dn * (zh * _sigmoid(zh))
        o_ref[:, h * dv:(h + 1) * dv] = o.astype(o_ref.dtype)

    @pl.when(c_id == pl.num_programs(1) - 1)
    def _():
        s_out_ref[...] = st_ref[...]


def _gdn(hmain, bg, buf8, s0, cw8, dn, layer, *, batch, rows_per_seq, n_valid, n_heads, dk, dv,
         conv_w, off_conv, off_z, out_dtype):
    m = hmain.shape[0]
    width = n_heads * dk
    cin = min(rows_per_seq, GDN_CHUNK)
    n_chunks = rows_per_seq // cin
    assert off_conv % (3 * width) == 0 and off_z % width == 0 and dk == dv == LANE
    return pl.pallas_call(
        functools.partial(_gdn_kernel, cin=cin, n_valid=n_valid, n_heads=n_heads, dk=dk, dv=dv,
                          conv_w=conv_w),
        grid=(batch, n_chunks),
        in_specs=[pl.BlockSpec((cin, 3 * width), lambda b, c: (b * n_chunks + c, off_conv // (3 * width))),
                  pl.BlockSpec((cin, width), lambda b, c: (b * n_chunks + c, off_z // width)),
                  pl.BlockSpec((cin, LANE), lambda b, c: (b * n_chunks + c, 0)),
                  pl.BlockSpec((None, SUBLANE, 3 * width), lambda b, c: (b, 0, 0)),
                  pl.BlockSpec((None, n_heads, dk, dv), lambda b, c: (b, 0, 0, 0)),
                  pl.BlockSpec((None, SUBLANE, 3 * width), lambda b, c: (layer, 0, 0)),
                  pl.BlockSpec((None, 1, dv), lambda b, c: (layer, 0, 0))],
        out_specs=[pl.BlockSpec((cin, width), lambda b, c: (b * n_chunks + c, 0)),
                   pl.BlockSpec((None, n_heads, dk, dv), lambda b, c: (b, 0, 0, 0))],
        out_shape=[jax.ShapeDtypeStruct((m, width), out_dtype),
                   jax.ShapeDtypeStruct((batch, n_heads, dk, dv), F32)],
        scratch_shapes=[pltpu.VMEM((SUBLANE, 3 * width), F32),
                        pltpu.VMEM((n_heads, dk, dv), F32)],
        compiler_params=_cparams(("parallel", "arbitrary"), 48),
        name="gdn",
    )(hmain, hmain, bg, buf8, s0, cw8, dn)


def _merge_kernel(oa_ref, od_ref, wa_ref, wb_ref, ga_ref, gb_ref, o_ref):
    a = _bdot(oa_ref[...], wa_ref[...])
    b = _bdot(od_ref[...], wb_ref[...])
    o_ref[...] = (ga_ref[...] * a + gb_ref[...] * b).astype(o_ref.dtype)


def _merge(oa, od, w_ba, w_bb, sg, layer, *, tm, tn, out_dtype):
    m, wa = oa.shape
    wb = od.shape[1]
    d = w_ba.shape[2]
    nj = d // tn
    return pl.pallas_call(
        _merge_kernel,
        grid=(m // tm, nj),
        in_specs=[pl.BlockSpec((tm, wa), lambda i, j: (i, 0)),
                  pl.BlockSpec((tm, wb), lambda i, j: (i, 0)),
                  pl.BlockSpec((None, wa, tn), lambda i, j: (layer, 0, j)),
                  pl.BlockSpec((None, wb, tn), lambda i, j: (layer, 0, j)),
                  pl.BlockSpec((tm, tn), lambda i, j: (i, j)),
                  pl.BlockSpec((tm, tn), lambda i, j: (i, nj + j))],
        out_specs=pl.BlockSpec((tm, tn), lambda i, j: (i, j)),
        out_shape=jax.ShapeDtypeStruct((m, d), out_dtype),
        compiler_params=_cparams(("parallel", "arbitrary"), 48),
        name="merge",
    )(oa, od, w_ba, w_bb, sg, sg)


def _layernorm_store(y, g_ref, b_ref, of_ref, ob_ref):
    mu = jnp.mean(y, axis=-1, keepdims=True)
    yc = y - mu
    var = jnp.mean(yc * yc, axis=-1, keepdims=True)
    out = yc * lax.rsqrt(var + LN_EPS) * g_ref[...] + b_ref[...]
    of_ref[...] = out
    ob_ref[...] = out.astype(ob_ref.dtype)


def _proj_ln_kernel(x_ref, w_ref, r_ref, g_ref, b_ref, of_ref, ob_ref, acc_ref, *, alpha):
    k = pl.program_id(1)

    @pl.when(k == 0)
    def _():
        acc_ref[...] = jnp.zeros(acc_ref.shape, F32)

    acc_ref[...] += _bdot(x_ref[...], w_ref[...])

    @pl.when(k == pl.num_programs(1) - 1)
    def _():
        _layernorm_store(alpha * r_ref[...] + acc_ref[...], g_ref, b_ref, of_ref, ob_ref)


def _proj_ln(x, w, resid, ln_g, ln_b, layer, *, alpha, tm, tk, act_dtype):
    m, kdim = x.shape
    d = w.shape[2]
    vec = pl.BlockSpec((None, 1, d), lambda i, k: (layer, 0, 0))
    row = pl.BlockSpec((tm, d), lambda i, k: (i, 0))
    return pl.pallas_call(
        functools.partial(_proj_ln_kernel, alpha=alpha),
        grid=(m // tm, kdim // tk),
        in_specs=[pl.BlockSpec((tm, tk), lambda i, k: (i, k)),
                  pl.BlockSpec((None, tk, d), lambda i, k: (layer, k, 0)),
                  row, vec, vec],
        out_specs=[row, row],
        out_shape=[jax.ShapeDtypeStruct((m, d), F32), jax.ShapeDtypeStruct((m, d), act_dtype)],
        scratch_shapes=[pltpu.VMEM((tm, d), F32)],
        compiler_params=_cparams(("parallel", "arbitrary"), 56),
        name="proj_ln",
    )(x, w, resid, ln_g, ln_b)


def _mlp_ln_kernel(x_ref, wu_ref, wd_ref, r_ref, g_ref, b_ref, of_ref, ob_ref, acc_ref, *, alpha):
    f = pl.program_id(1)

    @pl.when(f == 0)
    def _():
        acc_ref[...] = jnp.zeros(acc_ref.shape, F32)

    hid = jnp.maximum(_bdot(x_ref[...], wu_ref[...]), 0.0)
    acc_ref[...] += _bdot(hid * hid, wd_ref[...])

    @pl.when(f == pl.num_programs(1) - 1)
    def _():
        _layernorm_store(alpha * r_ref[...] + acc_ref[...], g_ref, b_ref, of_ref, ob_ref)


def _mlp_ln(x, w_up, w_down, resid, ln_g, ln_b, layer, *, alpha, tm, tf, act_dtype):
    m, d = x.shape
    dff = w_up.shape[2]
    vec = pl.BlockSpec((None, 1, d), lambda i, f: (layer, 0, 0))
    row = pl.BlockSpec((tm, d), lambda i, f: (i, 0))
    return pl.pallas_call(
        functools.partial(_mlp_ln_kernel, alpha=alpha),
        grid=(m // tm, dff // tf),
        in_specs=[row,
                  pl.BlockSpec((None, d, tf), lambda i, f: (layer, 0, f)),
                  pl.BlockSpec((None, tf, d), lambda i, f: (layer, f, 0)),
                  row, vec, vec],
        out_specs=[row, row],
        out_shape=[jax.ShapeDtypeStruct((m, d), F32), jax.ShapeDtypeStruct((m, d), act_dtype)],
        scratch_shapes=[pltpu.VMEM((tm, d), F32)],
        compiler_params=_cparams(("parallel", "arbitrary"), 56),
        name="mlp_ln",
    )(x, w_up, w_down, resid, ln_g, ln_b)


def _rope_tables(pos, hd, rot):
    half = rot // 2
    inv_freq = ROPE_THETA ** (-jnp.arange(half, dtype=F32) * 2.0 / rot)
    ang = pos.astype(F32)[:, None] * inv_freq[None, :]
    cos, sin = jnp.cos(ang), jnp.sin(ang)
    lane = jnp.arange(LANE)
    within = lane % hd
    idx = within % half
    cos_l, sin_l = cos[:, idx], sin[:, idx]
    c = jnp.where(within < rot, cos_l, 1.0)
    sa = jnp.where(within < half, -sin_l, 0.0)
    sb = jnp.where((within >= half) & (within < rot), sin_l, 0.0)
    return c, sa, sb


def _lane_vec(v):
    return jnp.broadcast_to(v.astype(F32)[:, None, None], (v.shape[0], 1, LANE))


def kernel(x_prompt, x_sample, cache_k, cache_v, state_ssm, state_conv, page_table, w_in, conv_w, a_log, dt_bias, delta_norm_w, lambda_q1, lambda_k1, lambda_q2, lambda_k2, attn_norm_w, w_branch_a, w_branch_b, w_out, ln1_g, ln1_b, w_mlp_up, w_mlp_down, ln2_g, ln2_b):
    depth, d_model, in_cols = w_in.shape
    bp, sp, _ = x_prompt.shape
    bs, ss, _ = x_sample.shape
    _, n_pool, page, ha, _, hd = cache_k.shape
    _, _, hdn, dk, dv = state_ssm.shape
    cw_len = conv_w.shape[1]
    n_pages = page_table.shape[1]
    past_len = n_pages * page
    qk_w = ha * 2 * hd
    wa = ha * cache_v.shape[-1]
    conv_ch = conv_w.shape[2]
    wdv = hdn * dv
    rot = hd // 4
    off_k, off_v, off_conv = qk_w, 2 * qk_w, 2 * qk_w + wa
    off_z = off_conv + conv_ch
    n_main = off_z + wdv
    off_gate = n_main + 2 * hdn
    assert in_cols == off_gate + 2 * d_model and 2 * hd == LANE and cache_v.shape[-1] == LANE
    assert qk_w == wa == wdv and conv_ch == 3 * wdv and cw_len <= SUBLANE
    alpha = (2 * depth) ** 0.25
    rows_s = SUBLANE

    w_gates = w_in[:, :, off_gate:]
    w_small = jnp.pad(w_in[:, :, n_main:off_gate], ((0, 0), (0, 0), (0, LANE - 2 * hdn)))
    alog = jnp.pad(a_log.astype(F32), ((0, 0), (hdn, LANE - 2 * hdn)))[:, None, :]
    dtb = jnp.pad(dt_bias.astype(F32), ((0, 0), (hdn, LANE - 2 * hdn)))[:, None, :]
    cw8 = jnp.pad(conv_w.astype(F32), ((0, 0), (0, SUBLANE - cw_len), (0, 0)))
    dn = delta_norm_w.astype(F32)[:, None, :]
    nw = attn_norm_w.astype(F32)[:, None, :]
    vec3 = lambda a: a.astype(F32)[:, None, :]
    g1, b1, g2, b2 = vec3(ln1_g), vec3(ln1_b), vec3(ln2_g), vec3(ln2_b)
    lam_init = jnp.asarray([0.8 - 0.6 * math.exp(-0.3 * l) for l in range(depth)], F32)
    lam = (jnp.exp(jnp.sum(lambda_q1.astype(F32) * lambda_k1.astype(F32), axis=-1))
           - jnp.exp(jnp.sum(lambda_q2.astype(F32) * lambda_k2.astype(F32), axis=-1)) + lam_init)
    lamv = _lane_vec(lam)

    tabs_p = _rope_tables(jnp.tile(jnp.arange(sp), bp), hd, rot)
    tabs_s = _rope_tables(jnp.tile(past_len + jnp.arange(rows_s), bs), hd, rot)
    ck = cache_k.reshape(depth, n_pool, page, qk_w)
    cv = cache_v.reshape(depth, n_pool, page, wa)

    mp = bp * sp
    ms = bs * rows_s
    xs_pad = jnp.pad(x_sample, ((0, 0), (0, rows_s - ss), (0, 0))).reshape(ms, d_model)
    xp = x_prompt.reshape(mp, d_model)
    state = {"p": (xp, xp.astype(BF16)), "s": (xs_pad, xs_pad)}
    buf_p = jnp.zeros((bp, SUBLANE, conv_ch), F32)
    s0_p = jnp.zeros((bp, hdn, dk, dv), F32)

    tn_main = _pick(math.gcd(qk_w, n_main), (1024, 512, 256, 128))
    tn_d = _pick(d_model, (512, 256, 128))
    tk_d = _pick(d_model, (512, 256, 128))
    tf = _pick(w_mlp_up.shape[2], (512, 256, 128))
    tq = _pick(sp, (256, 128))
    pp = _pick(n_pages, (8, 4, 2, 1))
    cfg = {"p": dict(tm_big=_pick(mp, (1024, 512, 256, 128)), tm_ln=_pick(mp, (512, 256, 128)),
                     act=BF16),
           "s": dict(tm_big=ms, tm_ln=ms, act=F32)}

    outs = {"p": ([], [], [], []), "s": ([], [], [], [])}
    for l in range(depth):
        post = 1.0 - (0.8 - 0.6 * math.exp(-0.3 * l))
        for grp in ("p", "s"):
            xf, xb = state[grp]
            c = cfg[grp]
            tm = c["tm_big"]
            tabs = tabs_p if grp == "p" else tabs_s
            hmain = _inproj(xb, w_in, l, tabs, n_main=n_main, n_rope=off_v, half=rot // 2,
                            tm=tm, tn=tn_main)
            sg = _gates(xb, w_gates, l, tm=tm, tn=tn_d)
            bg = _small(xb, w_small, l, alog, dtb, n_heads=hdn, tm=tm)
            if grp == "p":
                oa = _attn_prompt(hmain, lamv, nw, l, batch=bp, seq=sp, n_heads=ha, hd=hd,
                                  post_scale=post, tq=tq, tk=tq, out_dtype=BF16)
                od, ssm = _gdn(hmain, bg, buf_p, s0_p, cw8, dn, l, batch=bp, rows_per_seq=sp,
                               n_valid=sp, n_heads=hdn, dk=dk, dv=dv, conv_w=cw_len,
                               off_conv=off_conv, off_z=off_z, out_dtype=BF16)
                h3 = hmain.reshape(bp, sp, n_main)
                k_new = h3[:, :, off_k:off_v]
                v_new = h3[:, :, off_v:off_conv]
                conv_new = h3[:, sp - (cw_len - 1):, off_conv:off_z]
            else:
                oa = _attn_decode(hmain, ck, cv, page_table, lamv, nw, l, n_heads=ha, hd=hd,
                                  n_tok=ss, post_scale=post, pp=pp)
                buf_s = jnp.pad(state_conv[l].astype(F32),
                                ((0, 0), (SUBLANE - (cw_len - 1), 0), (0, 0)))
                od, ssm = _gdn(hmain, bg, buf_s, state_ssm[l].astype(F32), cw8, dn, l, batch=bs,
                               rows_per_seq=rows_s, n_valid=ss, n_heads=hdn, dk=dk, dv=dv,
                               conv_w=cw_len, off_conv=off_conv, off_z=off_z, out_dtype=F32)
                h3 = hmain.reshape(bs, rows_s, n_main)
                k_new = h3[:, :ss, off_k:off_v]
                v_new = h3[:, :ss, off_v:off_conv]
                conv_new = h3[:, ss - (cw_len - 1):ss, off_conv:off_z]
            mixed = _merge(oa, od, w_branch_a, w_branch_b, sg, l, tm=tm, tn=tn_d, out_dtype=c["act"])
            x1f, x1b = _proj_ln(mixed, w_out, xf, g1, b1, l, alpha=alpha, tm=c["tm_ln"], tk=tk_d,
                                act_dtype=c["act"])
            x2f, x2b = _mlp_ln(x1b, w_mlp_up, w_mlp_down, x1f, g2, b2, l, alpha=alpha,
                               tm=c["tm_ln"], tf=tf, act_dtype=c["act"])
            state[grp] = (x2f, x2b)
            ko, vo, so, co = outs[grp]
            nb = k_new.shape[0]
            ko.append(k_new.reshape(nb, -1, ha, 2, hd))
            vo.append(v_new.reshape(nb, -1, ha, cache_v.shape[-1]))
            so.append(ssm)
            co.append(conv_new)

    y_p = state["p"][0].reshape(bp, sp, d_model)
    y_s = state["s"][0].reshape(bs, rows_s, d_model)[:, :ss]
    kp, vp, sp_l, cp = (jnp.stack(t) for t in outs["p"])
    ks, vs, ss_l, cs = (jnp.stack(t) for t in outs["s"])
    return (y_p, y_s, kp, vp, sp_l, cp, ks, vs, ss_l, cs)
```

```python
import functools
import math

import jax
import jax.numpy as jnp
from jax import lax
from jax.experimental import pallas as pl
from jax.experimental.pallas import tpu as pltpu

F32 = jnp.float32
BF16 = jnp.bfloat16
LANE = 128
SUBLANE = 8
MIB = 1024 * 1024

ROPE_THETA = 500000.0
LN_EPS = 1e-5
RMS_EPS = 1e-6
L2_EPS = 1e-6
GDN_CHUNK = 128
NEG_BIG = -1e30


def _cparams(sem, vmem_mib):
    return pltpu.CompilerParams(dimension_semantics=sem, vmem_limit_bytes=vmem_mib * MIB)


def _pick(n, cands):
    for c in cands:
        if n % c == 0:
            return c
    raise ValueError(f"no tile for {n} in {cands}")


def _log2(n):
    assert n & (n - 1) == 0
    return n.bit_length() - 1


def _sigmoid(x):
    return 1.0 / (1.0 + jnp.exp(-x))


def _bdot(a, b):
    return jnp.dot(a.astype(BF16), b.astype(BF16), preferred_element_type=F32)


def _bdot_nt(a, b):
    return lax.dot_general(a.astype(BF16), b.astype(BF16), (((1,), (1,)), ((), ())),
                           preferred_element_type=F32)


def _split3(a):
    a1 = a.astype(BF16)
    r1 = a - a1.astype(F32)
    a2 = r1.astype(BF16)
    a3 = (r1 - a2.astype(F32)).astype(BF16)
    return a1, a2, a3


def _dot_exact_lhs(mask_bf16, b):
    d = functools.partial(jnp.dot, preferred_element_type=F32)
    b1, b2, b3 = _split3(b)
    return d(mask_bf16, b1) + (d(mask_bf16, b2) + d(mask_bf16, b3))


def _dot_exact_rhs(a, mask_bf16):
    d = functools.partial(jnp.dot, preferred_element_type=F32)
    a1, a2, a3 = _split3(a)
    return d(a1, mask_bf16) + (d(a2, mask_bf16) + d(a3, mask_bf16))


def _lane_bcast(x, j):
    return jnp.broadcast_to(x[:, j:j + 1], x.shape)


def _inproj_kernel(x_ref, w_ref, c_ref, sa_ref, sb_ref, o_ref, xb_ref, *, n_rope_blocks, tn, half):
    j = pl.program_id(1)

    @pl.when(j == 0)
    def _():
        xb_ref[...] = x_ref[...].astype(BF16)

    acc = _bdot_nt(xb_ref[...], w_ref[...])

    @pl.when(j >= n_rope_blocks)
    def _():
        o_ref[...] = acc

    @pl.when(j < n_rope_blocks)
    def _():
        c, sa, sb = c_ref[...], sa_ref[...], sb_ref[...]
        for g in range(tn // LANE):
            xg = acc[:, g * LANE:(g + 1) * LANE]
            o_ref[:, g * LANE:(g + 1) * LANE] = (
                xg * c + pltpu.roll(xg, LANE - half, 1) * sa + pltpu.roll(xg, half, 1) * sb)


def _inproj(x, w_t, layer, tabs, *, n_main, n_rope, half, tm, tn):
    m, d = x.shape
    c, sa, sb = tabs
    tab_spec = pl.BlockSpec((tm, LANE), lambda i, j: (i, 0))
    return pl.pallas_call(
        functools.partial(_inproj_kernel, n_rope_blocks=n_rope // tn, tn=tn, half=half),
        grid=(m // tm, n_main // tn),
        in_specs=[pl.BlockSpec((tm, d), lambda i, j: (i, 0)),
                  pl.BlockSpec((None, tn, d), lambda i, j: (layer, j, 0)),
                  tab_spec, tab_spec, tab_spec],
        out_specs=pl.BlockSpec((tm, tn), lambda i, j: (i, j)),
        out_shape=jax.ShapeDtypeStruct((m, n_main), F32),
        scratch_shapes=[pltpu.VMEM((tm, d), BF16)],
        compiler_params=_cparams(("parallel", "arbitrary"), 56),
        name="inproj",
    )(x, w_t, c, sa, sb)


def _gates_kernel(x_ref, wa_ref, wb_ref, o_ref, xb_ref, *, skip):
    @pl.when(pl.program_id(1) == 0)
    def _():
        xb_ref[...] = x_ref[...].astype(BF16)

    w = jnp.concatenate([wa_ref[skip:, :], wb_ref[...]], axis=0)
    o_ref[...] = _sigmoid(_bdot_nt(xb_ref[...], w))


def _gates(x, w_t, layer, *, off_small, skip, n_out, tm, tn):
    m, d = x.shape
    assert off_small % tn == 0 and tn % skip == 0 and off_small % skip == 0
    return pl.pallas_call(
        functools.partial(_gates_kernel, skip=skip),
        grid=(m // tm, n_out // tn),
        in_specs=[pl.BlockSpec((tm, d), lambda i, j: (i, 0)),
                  pl.BlockSpec((None, tn, d), lambda i, j: (layer, off_small // tn + j, 0)),
                  pl.BlockSpec((None, skip, d),
                               lambda i, j: (layer, (off_small + (j + 1) * tn) // skip, 0))],
        out_specs=pl.BlockSpec((tm, tn), lambda i, j: (i, j)),
        out_shape=jax.ShapeDtypeStruct((m, n_out), F32),
        scratch_shapes=[pltpu.VMEM((tm, d), BF16)],
        compiler_params=_cparams(("parallel", "arbitrary"), 48),
        name="gates",
    )(x, w_t, w_t)


def _small_kernel(x_ref, w_ref, alog_ref, dtb_ref, o_ref, *, n_heads):
    w = w_ref[...]
    w = jnp.concatenate([w, jnp.zeros((LANE - w.shape[0], w.shape[1]), w.dtype)], axis=0)
    h = _bdot_nt(x_ref[...], w)
    lane = lax.broadcasted_iota(jnp.int32, h.shape, 1)
    beta = _sigmoid(h)
    t = h + dtb_ref[...]
    softplus = jnp.maximum(t, 0.0) + jnp.log1p(jnp.exp(-jnp.abs(t)))
    g = -jnp.exp(alog_ref[...]) * softplus
    o_ref[...] = jnp.where(lane < n_heads, beta, jnp.where(lane < 2 * n_heads, g, 0.0))


def _small(x, w_t, layer, alog, dtb, *, off_small, n_heads, tm):
    m, d = x.shape
    rows = 2 * n_heads
    assert off_small % rows == 0 and rows % SUBLANE == 0
    vec = pl.BlockSpec((None, 1, LANE), lambda i: (layer, 0, 0))
    return pl.pallas_call(
        functools.partial(_small_kernel, n_heads=n_heads),
        grid=(m // tm,),
        in_specs=[pl.BlockSpec((tm, d), lambda i: (i, 0)),
                  pl.BlockSpec((None, rows, d), lambda i: (layer, off_small // rows, 0)),
                  vec, vec],
        out_specs=pl.BlockSpec((tm, LANE), lambda i: (i, 0)),
        out_shape=jax.ShapeDtypeStruct((m, LANE), F32),
        compiler_params=_cparams(("parallel",), 32),
        name="small",
    )(x, w_t, alog, dtb)


def _rms_head(o, nw, post_scale):
    return o * lax.rsqrt(jnp.mean(o * o, axis=-1, keepdims=True) + RMS_EPS) * nw * post_scale


def _attn_prompt_kernel(q_ref, k_ref, v_ref, lam_ref, nw_ref, o_ref, qm_ref,
                        *, tq, tk, sub, hd, scale, post_scale):
    qi = pl.program_id(2)
    q = q_ref[...] * scale
    lane = lax.broadcasted_iota(jnp.int32, q.shape, 1)
    qm_ref[0] = jnp.where(lane < hd, q, 0.0).astype(BF16)
    qm_ref[1] = jnp.where(lane >= hd, q, 0.0).astype(BF16)
    row = lax.broadcasted_iota(jnp.int32, (sub, tk), 0)
    col = lax.broadcasted_iota(jnp.int32, (sub, tk), 1)
    nr = tq // sub
    parts = [(mp, r) for mp in range(2) for r in range(nr)]
    ones = jnp.ones((tk, LANE), BF16)
    n_full = (qi * tq) // tk

    def block(kb, d, carry):
        ms, accs = carry
        start = pl.multiple_of(kb * tk, tk)
        k = k_ref[pl.ds(start, tk), :].astype(BF16)
        v1 = jnp.concatenate([v_ref[pl.ds(start, tk), :].astype(BF16), ones], axis=1)
        if d is None:
            live = list(range(len(parts)))
            masked = set()
        else:
            live = [i for i, (_, r) in enumerate(parts) if (r + 1) * sub - 1 >= d * tk]
            masked = {i for i in live if parts[i][1] * sub < (d + 1) * tk - 1}
        s_l = {i: _bdot_nt(qm_ref[parts[i][0], pl.ds(parts[i][1] * sub, sub), :], k) for i in live}
        for i in masked:
            s_l[i] = jnp.where(col + d * tk <= row + parts[i][1] * sub, s_l[i], NEG_BIG)
        mn_l = {i: jnp.maximum(ms[i], jnp.max(s_l[i], axis=-1, keepdims=True)) for i in live}
        p_l = {i: jnp.exp(s_l[i] - mn_l[i]).astype(BF16) for i in live}
        al_l = {i: jnp.exp(ms[i] - mn_l[i]) for i in live}
        pv_l = {i: jnp.dot(p_l[i], v1, preferred_element_type=F32) for i in live}
        ms = tuple(mn_l.get(i, ms[i]) for i in range(len(parts)))
        accs = tuple(al_l[i] * accs[i] + pv_l[i] if i in pv_l else accs[i] for i in range(len(parts)))
        return ms, accs

    carry = (tuple(jnp.full((sub, 1), NEG_BIG, F32) for _ in parts),
             tuple(jnp.zeros((sub, 2 * LANE), F32) for _ in parts))
    carry = lax.fori_loop(0, n_full, lambda kb, c: block(kb, None, c), carry)
    for d in range(tq // tk):
        carry = block(n_full + d, d, carry)
    accs = carry[1]
    num = [jnp.concatenate([accs[mp * nr + r][:, :LANE] for r in range(nr)], axis=0) for mp in range(2)]
    den = [jnp.concatenate([accs[mp * nr + r][:, LANE:] for r in range(nr)], axis=0) for mp in range(2)]
    o = num[0] / den[0] - lam_ref[...] * (num[1] / den[1])
    o_ref[...] = _rms_head(o, nw_ref[...], post_scale).astype(o_ref.dtype)


def _attn_prompt(hmain, lamv, nw, layer, *, batch, seq, n_heads, hd, post_scale, tq, tk, out_dtype):
    m = hmain.shape[0]
    nq = seq // tq
    sub = min(tk, LANE)
    assert tq % tk == 0 and tk % sub == 0
    vec = pl.BlockSpec((None, 1, LANE), lambda b, h, i: (layer, 0, 0))
    return pl.pallas_call(
        functools.partial(_attn_prompt_kernel, tq=tq, tk=tk, sub=sub, hd=hd, scale=hd ** -0.5,
                          post_scale=post_scale),
        grid=(batch, n_heads, nq),
        in_specs=[pl.BlockSpec((tq, LANE), lambda b, h, i: (b * nq + i, h)),
                  pl.BlockSpec((seq, LANE), lambda b, h, i: (b, n_heads + h)),
                  pl.BlockSpec((seq, LANE), lambda b, h, i: (b, 2 * n_heads + h)),
                  vec, vec],
        out_specs=pl.BlockSpec((tq, LANE), lambda b, h, i: (b * nq + i, h)),
        out_shape=jax.ShapeDtypeStruct((m, n_heads * LANE), out_dtype),
        scratch_shapes=[pltpu.VMEM((2, tq, LANE), BF16)],
        compiler_params=_cparams(("parallel", "parallel", "arbitrary"), 32),
        name="attn_prompt",
    )(hmain, hmain, hmain, lamv, nw)


def _attn_decode_kernel(pt_ref, q_ref, kn_ref, vn_ref, lam_ref, nw_ref, *rest,
                        pp, n_heads, hd, n_tok, page, scale, post_scale):
    k_refs, v_refs = rest[:pp], rest[pp:2 * pp]
    o_ref = rest[2 * pp]
    qbd_ref, m_ref, l_ref, acc_ref, kpad_ref, vpad_ref = rest[2 * pp + 1:]
    p_id = pl.program_id(1)
    rows = n_heads * SUBLANE
    width = n_heads * LANE

    def update(s_parts, v_parts):
        s = jnp.concatenate(s_parts, axis=1) if len(s_parts) > 1 else s_parts[0]
        m_old = m_ref[:, :1]
        mn = jnp.maximum(m_old, jnp.max(s, axis=-1, keepdims=True))
        p = jnp.exp(s - mn)
        al = jnp.exp(m_old - mn)
        l_ref[...] = jnp.broadcast_to(al * l_ref[:, :1] + jnp.sum(p, axis=-1, keepdims=True), (rows, LANE))
        m_ref[...] = jnp.broadcast_to(mn, (rows, LANE))
        pv = None
        for i, v in enumerate(v_parts):
            t = jnp.dot(p[:, i * page:(i + 1) * page].astype(BF16), v, preferred_element_type=F32)
            pv = t if pv is None else pv + t
        acc_ref[...] = al * acc_ref[...] + pv

    @pl.when(p_id == 0)
    def _():
        q = q_ref[...] * scale
        r8 = lax.broadcasted_iota(jnp.int32, q.shape, 0)
        q8 = jnp.where(r8 < n_tok, q, pltpu.roll(q, n_tok, 0))
        qt = jnp.concatenate([q8] * n_heads, axis=0)
        r = lax.broadcasted_iota(jnp.int32, qt.shape, 0)
        c = lax.broadcasted_iota(jnp.int32, qt.shape, 1)
        sel = (c >> _log2(hd)) == (r >> _log2(SUBLANE)) * 2 + ((r & (SUBLANE - 1)) >> _log2(n_tok))
        qbd_ref[...] = jnp.where(sel, qt, 0.0).astype(BF16)
        m_ref[...] = jnp.full(m_ref.shape, NEG_BIG, F32)
        l_ref[...] = jnp.zeros(l_ref.shape, F32)
        acc_ref[...] = jnp.zeros(acc_ref.shape, F32)
        kpad_ref[...] = jnp.zeros(kpad_ref.shape, F32)
        vpad_ref[...] = jnp.zeros(vpad_ref.shape, F32)
        kpad_ref[0:SUBLANE, :] = kn_ref[...]
        vpad_ref[0:SUBLANE, :] = vn_ref[...]
        s = _bdot_nt(qbd_ref[...], kpad_ref[...])
        rr = lax.broadcasted_iota(jnp.int32, s.shape, 0)
        cc = lax.broadcasted_iota(jnp.int32, s.shape, 1)
        s = jnp.where(cc <= (rr & (n_tok - 1)), s, NEG_BIG)
        update([s], [vpad_ref[...].astype(BF16)])

    qbd = qbd_ref[...]
    s_parts = [_bdot(qbd, k_refs[i][...]) for i in range(pp)]
    v_parts = [jnp.concatenate([v_refs[i][pl.ds(h, page, stride=n_heads), :] for h in range(n_heads)],
                               axis=1).astype(BF16) for i in range(pp)]
    update(s_parts, v_parts)

    @pl.when(p_id == pl.num_programs(1) - 1)
    def _():
        accn = acc_ref[...] / l_ref[:, :1]
        c = lax.broadcasted_iota(jnp.int32, (SUBLANE, width), 1)
        res = jnp.zeros((SUBLANE, width), F32)
        for h in range(n_heads):
            res = res + jnp.where((c >> _log2(LANE)) == h, accn[h * SUBLANE:(h + 1) * SUBLANE, :], 0.0)
        o8 = res - lam_ref[...][:, :1] * pltpu.roll(res, SUBLANE - n_tok, 0)
        nw = nw_ref[...]
        for h in range(n_heads):
            o_ref[:, h * LANE:(h + 1) * LANE] = _rms_head(o8[:, h * LANE:(h + 1) * LANE], nw, post_scale)


def _attn_decode(hmain, cache_kt, cache_v2, page_table, lamv, nw, layer, *, n_heads, hd, n_tok,
                 post_scale, pp):
    m = hmain.shape[0]
    batch, n_pages = page_table.shape
    page = cache_kt.shape[3]
    width = n_heads * LANE
    rows = n_heads * SUBLANE
    assert 2 * n_tok == SUBLANE and page == LANE and n_pages % pp == 0
    assert cache_kt.shape[2] == width and cache_v2.shape[2:] == (page * n_heads, LANE)
    vec = pl.BlockSpec((None, 1, LANE), lambda b, p, pt: (layer, 0, 0))

    def page_spec(i, shape):
        return pl.BlockSpec((None, None) + shape, lambda b, p, pt: (layer, pt[b, p * pp + i], 0, 0))

    in_specs = ([pl.BlockSpec((SUBLANE, width), lambda b, p, pt: (b, 0)),
                 pl.BlockSpec((SUBLANE, width), lambda b, p, pt: (b, 1)),
                 pl.BlockSpec((SUBLANE, width), lambda b, p, pt: (b, 2)),
                 vec, vec]
                + [page_spec(i, (width, page)) for i in range(pp)]
                + [page_spec(i, (page * n_heads, LANE)) for i in range(pp)])
    grid_spec = pltpu.PrefetchScalarGridSpec(
        num_scalar_prefetch=1,
        grid=(batch, n_pages // pp),
        in_specs=in_specs,
        out_specs=pl.BlockSpec((SUBLANE, width), lambda b, p, pt: (b, 0)),
        scratch_shapes=[pltpu.VMEM((rows, width), BF16),
                        pltpu.VMEM((rows, LANE), F32),
                        pltpu.VMEM((rows, LANE), F32),
                        pltpu.VMEM((rows, width), F32),
                        pltpu.VMEM((page, width), F32),
                        pltpu.VMEM((page, width), F32)])
    return pl.pallas_call(
        functools.partial(_attn_decode_kernel, pp=pp, n_heads=n_heads, hd=hd, n_tok=n_tok, page=page,
                          scale=hd ** -0.5, post_scale=post_scale),
        grid_spec=grid_spec,
        out_shape=jax.ShapeDtypeStruct((m, width), F32),
        compiler_params=_cparams(("parallel", "arbitrary"), 48),
        name="attn_decode",
    )(page_table, hmain, hmain, hmain, lamv, nw, *([cache_kt] * pp), *([cache_v2] * pp))


def _gdn_kernel(u_ref, z_ref, bg_ref, buf_ref, s0_ref, cw_ref, dn_ref, o_ref, s_out_ref,
                halo_ref, st_ref, *, cin, n_valid, n_heads, dk, dv, conv_w):
    c_id = pl.program_id(1)
    C = GDN_CHUNK
    width = n_heads * dk
    heads = range(n_heads)

    @pl.when(c_id == 0)
    def _():
        halo_ref[...] = buf_ref[...]
        st_ref[...] = s0_ref[...]

    u = u_ref[...]
    ext = jnp.concatenate([halo_ref[...], u], axis=0)
    cw = cw_ref[...]
    conv = None
    for i in range(conv_w):
        sh = SUBLANE - (conv_w - 1) + i
        term = ext[sh:sh + cin, :] * cw[i:i + 1, :]
        conv = term if conv is None else conv + term
    halo_ref[...] = u[cin - SUBLANE:cin, :]
    act = conv * _sigmoid(conv)
    bg = bg_ref[...]
    if n_valid < cin:
        valid = lax.broadcasted_iota(jnp.int32, (cin, 1), 0) < n_valid
        act = jnp.where(valid, act, 0.0)
        bg = jnp.where(valid, bg, 0.0)

    def pad_rows(a):
        if cin == C:
            return a
        return jnp.concatenate([a, jnp.zeros((C - cin, a.shape[1]), a.dtype)], axis=0)

    ri = lax.broadcasted_iota(jnp.int32, (C, C), 0)
    ci = lax.broadcasted_iota(jnp.int32, (C, C), 1)
    causal = ri >= ci
    strict = ri > ci
    tril = jnp.where(causal, 1.0, 0.0).astype(BF16)
    triu = jnp.where(ri <= ci, 1.0, 0.0).astype(BF16)
    eye = jnp.where(ri == ci, 1.0, 0.0).astype(F32)
    n_levels = _log2(C)

    bgp = pad_rows(bg)
    g_col = _dot_exact_lhs(tril, bgp)
    g_row = _dot_exact_rhs(bgp.T, triu)
    g_last = g_col[C - 1:C, :]
    eg_all = jnp.exp(g_col)
    ekd_all = jnp.exp(g_last - g_col)
    egl_all = jnp.exp(g_last)

    q_l, k_l, low_l, qk_l, rhs_l = [], [], [], [], []
    for h in heads:
        q = act[:, h * dk:(h + 1) * dk]
        k = act[:, width + h * dk:width + (h + 1) * dk]
        v = act[:, 2 * width + h * dv:2 * width + (h + 1) * dv]
        q = pad_rows(q * lax.rsqrt(jnp.sum(q * q, axis=-1, keepdims=True) + L2_EPS) * (dk ** -0.5))
        k = pad_rows(k * lax.rsqrt(jnp.sum(k * k, axis=-1, keepdims=True) + L2_EPS))
        v = pad_rows(v)
        beta = _lane_bcast(bgp, h)
        gc = _lane_bcast(g_col, n_heads + h)
        gr = jnp.broadcast_to(g_row[n_heads + h:n_heads + h + 1, :], (C, C))
        decay = jnp.exp(jnp.where(causal, gc - gr, NEG_BIG))
        kb = k * beta
        kq = _bdot_nt(jnp.concatenate([kb, q], axis=0), k)
        low_l.append(kq[:C] * jnp.where(strict, decay, 0.0))
        qk_l.append(kq[C:] * decay)
        rhs_l.append(jnp.concatenate([v * beta, kb * _lane_bcast(eg_all, n_heads + h)], axis=1))
        q_l.append(q)
        k_l.append(k)

    def level_mask(lvl):
        return ((ri >> (lvl + 1)) == (ci >> (lvl + 1))) & ((ri >> lvl) != (ci >> lvl))

    m0 = level_mask(0)
    t_l = [eye - jnp.where(m0, low, 0.0) for low in low_l]
    for lvl in range(1, n_levels):
        mk = level_mask(lvl)
        et_l = [_bdot(jnp.where(mk, low, 0.0), t) for low, t in zip(low_l, t_l)]
        t_l = [t - _bdot(t, et) for t, et in zip(t_l, et_l)]

    sol_l = [_bdot(t, rhs) for t, rhs in zip(t_l, rhs_l)]
    z = z_ref[...]
    dn = dn_ref[...]
    for h in heads:
        val, kcd = sol_l[h][:, :dv], sol_l[h][:, dv:]
        st = st_ref[h]
        q_dec = q_l[h] * _lane_bcast(eg_all, n_heads + h)
        ks = _bdot(jnp.concatenate([kcd, q_dec], axis=0), st)
        v_new = val - ks[:C]
        o = ks[C:] + _bdot(qk_l[h], v_new)
        k_dec = k_l[h] * _lane_bcast(ekd_all, n_heads + h)
        st_ref[h] = st * egl_all[:, n_heads + h:n_heads + h + 1] + _bdot(k_dec.T, v_new)
        o = o[:cin, :]
        zh = z[:, h * dv:(h + 1) * dv]
        o = o * lax.rsqrt(jnp.mean(o * o, axis=-1, keepdims=True) + RMS_EPS) * dn * (zh * _sigmoid(zh))
        o_ref[:, h * dv:(h + 1) * dv] = o.astype(o_ref.dtype)

    @pl.when(c_id == pl.num_programs(1) - 1)
    def _():
        s_out_ref[...] = st_ref[...]


def _gdn(hmain, bg, buf8, s0, cw8, dn, layer, *, batch, rows_per_seq, n_valid, n_heads, dk, dv,
         conv_w, off_conv, off_z, out_dtype):
    m = hmain.shape[0]
    width = n_heads * dk
    cin = min(rows_per_seq, GDN_CHUNK)
    n_chunks = rows_per_seq // cin
    assert off_conv % (3 * width) == 0 and off_z % width == 0 and dk == dv == LANE == GDN_CHUNK
    return pl.pallas_call(
        functools.partial(_gdn_kernel, cin=cin, n_valid=n_valid, n_heads=n_heads, dk=dk, dv=dv,
                          conv_w=conv_w),
        grid=(batch, n_chunks),
        in_specs=[pl.BlockSpec((cin, 3 * width), lambda b, c: (b * n_chunks + c, off_conv // (3 * width))),
                  pl.BlockSpec((cin, width), lambda b, c: (b * n_chunks + c, off_z // width)),
                  pl.BlockSpec((cin, LANE), lambda b, c: (b * n_chunks + c, 0)),
                  pl.BlockSpec((None, SUBLANE, 3 * width), lambda b, c: (b, 0, 0)),
                  pl.BlockSpec((None, n_heads, dk, dv), lambda b, c: (b, 0, 0, 0)),
                  pl.BlockSpec((None, SUBLANE, 3 * width), lambda b, c: (layer, 0, 0)),
                  pl.BlockSpec((None, 1, dv), lambda b, c: (layer, 0, 0))],
        out_specs=[pl.BlockSpec((cin, width), lambda b, c: (b * n_chunks + c, 0)),
                   pl.BlockSpec((None, n_heads, dk, dv), lambda b, c: (b, 0, 0, 0))],
        out_shape=[jax.ShapeDtypeStruct((m, width), out_dtype),
                   jax.ShapeDtypeStruct((batch, n_heads, dk, dv), F32)],
        scratch_shapes=[pltpu.VMEM((SUBLANE, 3 * width), F32),
                        pltpu.VMEM((n_heads, dk, dv), F32)],
        compiler_params=_cparams(("parallel", "arbitrary"), 48),
        name="gdn",
    )(hmain, hmain, bg, buf8, s0, cw8, dn)


def _merge_kernel(oa_ref, od_ref, wa_ref, wb_ref, ga_ref, gb_ref, o_ref):
    a = _bdot(oa_ref[...], wa_ref[...])
    b = _bdot(od_ref[...], wb_ref[...])
    o_ref[...] = (ga_ref[...] * a + gb_ref[...] * b).astype(o_ref.dtype)


def _merge(oa, od, w_ba, w_bb, sg, layer, *, tm, tn, out_dtype):
    m, wa = oa.shape
    wb = od.shape[1]
    d = w_ba.shape[2]
    nj = d // tn
    return pl.pallas_call(
        _merge_kernel,
        grid=(m // tm, nj),
        in_specs=[pl.BlockSpec((tm, wa), lambda i, j: (i, 0)),
                  pl.BlockSpec((tm, wb), lambda i, j: (i, 0)),
                  pl.BlockSpec((None, wa, tn), lambda i, j: (layer, 0, j)),
                  pl.BlockSpec((None, wb, tn), lambda i, j: (layer, 0, j)),
                  pl.BlockSpec((tm, tn), lambda i, j: (i, j)),
                  pl.BlockSpec((tm, tn), lambda i, j: (i, nj + j))],
        out_specs=pl.BlockSpec((tm, tn), lambda i, j: (i, j)),
        out_shape=jax.ShapeDtypeStruct((m, d), out_dtype),
        compiler_params=_cparams(("parallel", "arbitrary"), 48),
        name="merge",
    )(oa, od, w_ba, w_bb, sg, sg)


def _layernorm(y, g_ref, b_ref):
    mu = jnp.mean(y, axis=-1, keepdims=True)
    yc = y - mu
    var = jnp.mean(yc * yc, axis=-1, keepdims=True)
    return yc * lax.rsqrt(var + LN_EPS) * g_ref[...] + b_ref[...]


def _proj_ln_kernel(x_ref, w_ref, r_ref, g_ref, b_ref, o_ref, *, alpha):
    k = pl.program_id(1)

    @pl.when(k == 0)
    def _():
        o_ref[...] = alpha * r_ref[...]

    o_ref[...] += _bdot(x_ref[...], w_ref[...])

    @pl.when(k == pl.num_programs(1) - 1)
    def _():
        o_ref[...] = _layernorm(o_ref[...], g_ref, b_ref)


def _proj_ln(x, w, resid, ln_g, ln_b, layer, *, alpha, tm, tk):
    m, kdim = x.shape
    d = w.shape[2]
    vec = pl.BlockSpec((None, 1, d), lambda i, k: (layer, 0, 0))
    row = pl.BlockSpec((tm, d), lambda i, k: (i, 0))
    return pl.pallas_call(
        functools.partial(_proj_ln_kernel, alpha=alpha),
        grid=(m // tm, kdim // tk),
        in_specs=[pl.BlockSpec((tm, tk), lambda i, k: (i, k)),
                  pl.BlockSpec((None, tk, d), lambda i, k: (layer, k, 0)),
                  row, vec, vec],
        out_specs=row,
        out_shape=jax.ShapeDtypeStruct((m, d), F32),
        compiler_params=_cparams(("parallel", "arbitrary"), 56),
        name="proj_ln",
    )(x, w, resid, ln_g, ln_b)


def _mlp_ln_kernel(x_ref, wu_ref, wd_ref, g_ref, b_ref, o_ref, xb_ref, *, alpha):
    f = pl.program_id(1)

    @pl.when(f == 0)
    def _():
        x = x_ref[...]
        xb_ref[...] = x.astype(BF16)
        o_ref[...] = alpha * x

    hid = jnp.maximum(jnp.dot(xb_ref[...], wu_ref[...].astype(BF16), preferred_element_type=F32), 0.0)
    o_ref[...] += _bdot(hid * hid, wd_ref[...])

    @pl.when(f == pl.num_programs(1) - 1)
    def _():
        o_ref[...] = _layernorm(o_ref[...], g_ref, b_ref)


def _mlp_ln(x, w_up, w_down, ln_g, ln_b, layer, *, alpha, tm, tf):
    m, d = x.shape
    dff = w_up.shape[2]
    vec = pl.BlockSpec((None, 1, d), lambda i, f: (layer, 0, 0))
    row = pl.BlockSpec((tm, d), lambda i, f: (i, 0))
    return pl.pallas_call(
        functools.partial(_mlp_ln_kernel, alpha=alpha),
        grid=(m // tm, dff // tf),
        in_specs=[row,
                  pl.BlockSpec((None, d, tf), lambda i, f: (layer, 0, f)),
                  pl.BlockSpec((None, tf, d), lambda i, f: (layer, f, 0)),
                  vec, vec],
        out_specs=row,
        out_shape=jax.ShapeDtypeStruct((m, d), F32),
        scratch_shapes=[pltpu.VMEM((tm, d), BF16)],
        compiler_params=_cparams(("parallel", "arbitrary"), 58),
        name="mlp_ln",
    )(x, w_up, w_down, ln_g, ln_b)


def _rope_tables(pos, hd, rot):
    half = rot // 2
    inv_freq = ROPE_THETA ** (-jnp.arange(half, dtype=F32) * 2.0 / rot)
    ang = pos.astype(F32)[:, None] * inv_freq[None, :]
    cos, sin = jnp.cos(ang), jnp.sin(ang)
    lane = jnp.arange(LANE)
    within = lane % hd
    idx = within % half
    cos_l, sin_l = cos[:, idx], sin[:, idx]
    c = jnp.where(within < rot, cos_l, 1.0)
    sa = jnp.where(within < half, -sin_l, 0.0)
    sb = jnp.where((within >= half) & (within < rot), sin_l, 0.0)
    return c, sa, sb


def _lane_vec(v):
    return jnp.broadcast_to(v.astype(F32)[:, None, None], (v.shape[0], 1, LANE))


def kernel(x_prompt, x_sample, cache_k, cache_v, state_ssm, state_conv, page_table, w_in, conv_w, a_log, dt_bias, delta_norm_w, lambda_q1, lambda_k1, lambda_q2, lambda_k2, attn_norm_w, w_branch_a, w_branch_b, w_out, ln1_g, ln1_b, w_mlp_up, w_mlp_down, ln2_g, ln2_b):
    depth, d_model, in_cols = w_in.shape
    bp, sp, _ = x_prompt.shape
    bs, ss, _ = x_sample.shape
    _, n_pool, page, ha, _, hd = cache_k.shape
    _, _, hdn, dk, dv = state_ssm.shape
    dvh = cache_v.shape[-1]
    cw_len = conv_w.shape[1]
    n_pages = page_table.shape[1]
    past_len = n_pages * page
    qk_w = ha * 2 * hd
    wa = ha * dvh
    conv_ch = conv_w.shape[2]
    wdv = hdn * dv
    rot = hd // 4
    off_k, off_v, off_conv = qk_w, 2 * qk_w, 2 * qk_w + wa
    off_z = off_conv + conv_ch
    n_main = off_z + wdv
    off_gate = n_main + 2 * hdn
    assert in_cols == off_gate + 2 * d_model and 2 * hd == LANE and dvh == LANE
    assert qk_w == wa == wdv and conv_ch == 3 * wdv and cw_len <= SUBLANE
    alpha = (2 * depth) ** 0.25
    rows_s = SUBLANE

    w_t = jnp.swapaxes(w_in, 1, 2)
    cache_kt = jnp.transpose(cache_k, (0, 1, 3, 4, 5, 2)).reshape(depth, n_pool, qk_w, page)
    cache_v2 = cache_v.reshape(depth, n_pool, page * ha, dvh)

    alog = jnp.pad(a_log.astype(F32), ((0, 0), (hdn, LANE - 2 * hdn)))[:, None, :]
    dtb = jnp.pad(dt_bias.astype(F32), ((0, 0), (hdn, LANE - 2 * hdn)))[:, None, :]
    cw8 = jnp.pad(conv_w.astype(F32), ((0, 0), (0, SUBLANE - cw_len), (0, 0)))
    dn = delta_norm_w.astype(F32)[:, None, :]
    nw = attn_norm_w.astype(F32)[:, None, :]
    vec3 = lambda a: a.astype(F32)[:, None, :]
    g1, b1, g2, b2 = vec3(ln1_g), vec3(ln1_b), vec3(ln2_g), vec3(ln2_b)
    lam_init = jnp.asarray([0.8 - 0.6 * math.exp(-0.3 * l) for l in range(depth)], F32)
    lam = (jnp.exp(jnp.sum(lambda_q1.astype(F32) * lambda_k1.astype(F32), axis=-1))
           - jnp.exp(jnp.sum(lambda_q2.astype(F32) * lambda_k2.astype(F32), axis=-1)) + lam_init)
    lamv = _lane_vec(lam)

    tabs_p = _rope_tables(jnp.tile(jnp.arange(sp), bp), hd, rot)
    tabs_s = _rope_tables(jnp.tile(past_len + jnp.arange(rows_s), bs), hd, rot)

    mp = bp * sp
    ms = bs * rows_s
    xs_pad = jnp.pad(x_sample, ((0, 0), (0, rows_s - ss), (0, 0))).reshape(ms, d_model)
    xp = x_prompt.reshape(mp, d_model)
    state = {"p": xp, "s": xs_pad}
    buf_p = jnp.zeros((bp, SUBLANE, conv_ch), F32)
    s0_p = jnp.zeros((bp, hdn, dk, dv), F32)

    tn_main = _pick(math.gcd(qk_w, n_main), (1024, 512, 256, 128))
    tn_d = _pick(math.gcd(d_model, n_main), (512, 256, 128))
    tk_d = _pick(d_model, (512, 256, 128))
    tf = _pick(w_mlp_up.shape[2], (512, 256, 128))
    tq = _pick(sp, (512, 256, 128))
    tk_attn = min(tq, 256)
    pp = _pick(n_pages, (8, 4, 2, 1))
    cfg = {"p": dict(tm=_pick(mp, (1024, 512, 256, 128)), act=BF16), "s": dict(tm=ms, act=F32)}

    outs = {"p": ([], [], [], []), "s": ([], [], [], [])}
    for l in range(depth):
        post = 1.0 - (0.8 - 0.6 * math.exp(-0.3 * l))
        for grp in ("p", "s"):
            xb = state[grp]
            c = cfg[grp]
            tm = c["tm"]
            tabs = tabs_p if grp == "p" else tabs_s
            hmain = _inproj(xb, w_t, l, tabs, n_main=n_main, n_rope=off_v, half=rot // 2,
                            tm=tm, tn=tn_main)
            sg = _gates(xb, w_t, l, off_small=n_main, skip=2 * hdn, n_out=2 * d_model, tm=tm, tn=tn_d)
            bg = _small(xb, w_t, l, alog, dtb, off_small=n_main, n_heads=hdn, tm=tm)
            if grp == "p":
                oa = _attn_prompt(hmain, lamv, nw, l, batch=bp, seq=sp, n_heads=ha, hd=hd,
                                  post_scale=post, tq=tq, tk=tk_attn, out_dtype=BF16)
                od, ssm = _gdn(hmain, bg, buf_p, s0_p, cw8, dn, l, batch=bp, rows_per_seq=sp,
                               n_valid=sp, n_heads=hdn, dk=dk, dv=dv, conv_w=cw_len,
                               off_conv=off_conv, off_z=off_z, out_dtype=BF16)
                h3 = hmain.reshape(bp, sp, n_main)
                k_new = h3[:, :, off_k:off_v]
                v_new = h3[:, :, off_v:off_conv]
                conv_new = h3[:, sp - (cw_len - 1):, off_conv:off_z]
            else:
                oa = _attn_decode(hmain, cache_kt, cache_v2, page_table, lamv, nw, l, n_heads=ha,
                                  hd=hd, n_tok=ss, post_scale=post, pp=pp)
                buf_s = jnp.pad(state_conv[l].astype(F32),
                                ((0, 0), (SUBLANE - (cw_len - 1), 0), (0, 0)))
                od, ssm = _gdn(hmain, bg, buf_s, state_ssm[l].astype(F32), cw8, dn, l, batch=bs,
                               rows_per_seq=rows_s, n_valid=ss, n_heads=hdn, dk=dk, dv=dv,
                               conv_w=cw_len, off_conv=off_conv, off_z=off_z, out_dtype=F32)
                h3 = hmain.reshape(bs, rows_s, n_main)
                k_new = h3[:, :ss, off_k:off_v]
                v_new = h3[:, :ss, off_v:off_conv]
                conv_new = h3[:, ss - (cw_len - 1):ss, off_conv:off_z]
            mixed = _merge(oa, od, w_branch_a, w_branch_b, sg, l, tm=tm, tn=tn_d, out_dtype=c["act"])
            x1 = _proj_ln(mixed, w_out, xb, g1, b1, l, alpha=alpha, tm=tm, tk=tk_d)
            state[grp] = _mlp_ln(x1, w_mlp_up, w_mlp_down, g2, b2, l, alpha=alpha, tm=tm, tf=tf)
            ko, vo, so, co = outs[grp]
            nb = k_new.shape[0]
            ko.append(k_new.reshape(nb, -1, ha, 2, hd))
            vo.append(v_new.reshape(nb, -1, ha, dvh))
            so.append(ssm)
            co.append(conv_new)

    y_p = state["p"].reshape(bp, sp, d_model)
    y_s = state["s"].reshape(bs, rows_s, d_model)[:, :ss]
    kp, vp, sp_l, cp = (jnp.stack(t) for t in outs["p"])
    ks, vs, ss_l, cs = (jnp.stack(t) for t in outs["s"])
    return (y_p, y_s, kp, vp, sp_l, cp, ks, vs, ss_l, cs)
```

```python
import functools
import math

import jax
import jax.numpy as jnp
from jax import lax
from jax.experimental import pallas as pl
from jax.experimental.pallas import tpu as pltpu

F32 = jnp.float32
BF16 = jnp.bfloat16
LANE = 128
SUBLANE = 8
MIB = 1024 * 1024

ROPE_THETA = 500000.0
LN_EPS = 1e-5
RMS_EPS = 1e-6
L2_EPS = 1e-6
GDN_CHUNK = 128
NEG_BIG = -1e30


def _cparams(sem, vmem_mib):
    return pltpu.CompilerParams(dimension_semantics=sem, vmem_limit_bytes=vmem_mib * MIB)


def _pick(n, cands):
    for c in cands:
        if n % c == 0:
            return c
    raise ValueError(f"no tile for {n} in {cands}")


def _log2(n):
    assert n & (n - 1) == 0
    return n.bit_length() - 1


def _sigmoid(x):
    return 1.0 / (1.0 + jnp.exp(-x))


def _bdot(a, b):
    return jnp.dot(a.astype(BF16), b.astype(BF16), preferred_element_type=F32)


def _bdot_nt(a, b):
    return lax.dot_general(a.astype(BF16), b.astype(BF16), (((1,), (1,)), ((), ())),
                           preferred_element_type=F32)


def _split3(a):
    a1 = a.astype(BF16)
    r1 = a - a1.astype(F32)
    a2 = r1.astype(BF16)
    a3 = (r1 - a2.astype(F32)).astype(BF16)
    return a1, a2, a3


def _dot_exact_lhs(mask_bf16, b):
    d = functools.partial(jnp.dot, preferred_element_type=F32)
    b1, b2, b3 = _split3(b)
    return d(mask_bf16, b1) + (d(mask_bf16, b2) + d(mask_bf16, b3))


def _lane_bcast(x, j):
    return jnp.broadcast_to(x[:, j:j + 1], x.shape)


def _inproj_kernel(x_ref, w_ref, c_ref, sa_ref, sb_ref, o_ref, xb_ref, *, n_rope_blocks, tn, half):
    j = pl.program_id(1)

    @pl.when(j == 0)
    def _():
        xb_ref[...] = x_ref[...].astype(BF16)

    acc = _bdot_nt(xb_ref[...], w_ref[...])

    @pl.when(j >= n_rope_blocks)
    def _():
        o_ref[...] = acc

    @pl.when(j < n_rope_blocks)
    def _():
        c, sa, sb = c_ref[...], sa_ref[...], sb_ref[...]
        for g in range(tn // LANE):
            xg = acc[:, g * LANE:(g + 1) * LANE]
            o_ref[:, g * LANE:(g + 1) * LANE] = (
                xg * c + pltpu.roll(xg, LANE - half, 1) * sa + pltpu.roll(xg, half, 1) * sb)


def _inproj(x, w_t, layer, tabs, *, n_main, n_rope, half, tm, tn):
    m, d = x.shape
    c, sa, sb = tabs
    tab_spec = pl.BlockSpec((tm, LANE), lambda i, j: (i, 0))
    return pl.pallas_call(
        functools.partial(_inproj_kernel, n_rope_blocks=n_rope // tn, tn=tn, half=half),
        grid=(m // tm, n_main // tn),
        in_specs=[pl.BlockSpec((tm, d), lambda i, j: (i, 0)),
                  pl.BlockSpec((None, tn, d), lambda i, j: (layer, j, 0)),
                  tab_spec, tab_spec, tab_spec],
        out_specs=pl.BlockSpec((tm, tn), lambda i, j: (i, j)),
        out_shape=jax.ShapeDtypeStruct((m, n_main), F32),
        scratch_shapes=[pltpu.VMEM((tm, d), BF16)],
        compiler_params=_cparams(("parallel", "arbitrary"), 56),
        name="inproj",
    )(x, w_t, c, sa, sb)


def _gates_kernel(x_ref, wa_ref, wb_ref, alog_ref, dtb_ref, o_ref, bg_ref, xb_ref, *, skip, n_heads):
    j = pl.program_id(1)

    @pl.when(j == 0)
    def _():
        xb_ref[...] = x_ref[...].astype(BF16)
        w = wa_ref[0:skip, :]
        w = jnp.concatenate([w, jnp.zeros((LANE - skip, w.shape[1]), w.dtype)], axis=0)
        h = _bdot_nt(xb_ref[...], w)
        lane = lax.broadcasted_iota(jnp.int32, h.shape, 1)
        beta = _sigmoid(h)
        t = h + dtb_ref[...]
        softplus = jnp.maximum(t, 0.0) + jnp.log1p(jnp.exp(-jnp.abs(t)))
        g = -jnp.exp(alog_ref[...]) * softplus
        bg_ref[...] = jnp.where(lane < n_heads, beta, jnp.where(lane < 2 * n_heads, g, 0.0))

    w = jnp.concatenate([wa_ref[skip:, :], wb_ref[...]], axis=0)
    o_ref[...] = _sigmoid(_bdot_nt(xb_ref[...], w))


def _gates(x, w_t, layer, alog, dtb, *, off_small, n_heads, n_out, tm, tn):
    m, d = x.shape
    skip = 2 * n_heads
    assert off_small % tn == 0 and tn % skip == 0 and skip % SUBLANE == 0 and skip <= LANE
    vec = pl.BlockSpec((None, 1, LANE), lambda i, j: (layer, 0, 0))
    return pl.pallas_call(
        functools.partial(_gates_kernel, skip=skip, n_heads=n_heads),
        grid=(m // tm, n_out // tn),
        in_specs=[pl.BlockSpec((tm, d), lambda i, j: (i, 0)),
                  pl.BlockSpec((None, tn, d), lambda i, j: (layer, off_small // tn + j, 0)),
                  pl.BlockSpec((None, skip, d),
                               lambda i, j: (layer, (off_small + (j + 1) * tn) // skip, 0)),
                  vec, vec],
        out_specs=[pl.BlockSpec((tm, tn), lambda i, j: (i, j)),
                   pl.BlockSpec((tm, LANE), lambda i, j: (i, 0))],
        out_shape=[jax.ShapeDtypeStruct((m, n_out), F32), jax.ShapeDtypeStruct((m, LANE), F32)],
        scratch_shapes=[pltpu.VMEM((tm, d), BF16)],
        compiler_params=_cparams(("parallel", "arbitrary"), 48),
        name="gates",
    )(x, w_t, w_t, alog, dtb)


def _rms_head(o, nw, post_scale):
    return o * lax.rsqrt(jnp.mean(o * o, axis=-1, keepdims=True) + RMS_EPS) * nw * post_scale


def _attn_prompt_kernel(q_ref, k_ref, v_ref, lam_ref, nw_ref, o_ref, qm_ref,
                        *, tq, tk, sub, hd, scale, post_scale):
    qi = pl.program_id(2)
    q = q_ref[...] * scale
    lane = lax.broadcasted_iota(jnp.int32, q.shape, 1)
    qm_ref[0] = jnp.where(lane < hd, q, 0.0).astype(BF16)
    qm_ref[1] = jnp.where(lane >= hd, q, 0.0).astype(BF16)
    row = lax.broadcasted_iota(jnp.int32, (sub, tk), 0)
    col = lax.broadcasted_iota(jnp.int32, (sub, tk), 1)
    nr = tq // sub
    parts = [(mp, r) for mp in range(2) for r in range(nr)]
    ones = jnp.ones((tk, LANE), BF16)
    n_full = (qi * tq) // tk

    def block(kb, d, carry):
        ms, accs = carry
        start = pl.multiple_of(kb * tk, tk)
        k = k_ref[pl.ds(start, tk), :].astype(BF16)
        v1 = jnp.concatenate([v_ref[pl.ds(start, tk), :].astype(BF16), ones], axis=1)
        if d is None:
            live = list(range(len(parts)))
            masked = set()
        else:
            live = [i for i, (_, r) in enumerate(parts) if (r + 1) * sub - 1 >= d * tk]
            masked = {i for i in live if parts[i][1] * sub < (d + 1) * tk - 1}
        s_l = {i: _bdot_nt(qm_ref[parts[i][0], pl.ds(parts[i][1] * sub, sub), :], k) for i in live}
        for i in masked:
            s_l[i] = jnp.where(col + d * tk <= row + parts[i][1] * sub, s_l[i], NEG_BIG)
        mn_l = {i: jnp.maximum(ms[i], jnp.max(s_l[i], axis=-1, keepdims=True)) for i in live}
        p_l = {i: jnp.exp(s_l[i] - mn_l[i]).astype(BF16) for i in live}
        al_l = {i: jnp.exp(ms[i] - mn_l[i]) for i in live}
        pv_l = {i: jnp.dot(p_l[i], v1, preferred_element_type=F32) for i in live}
        ms = tuple(mn_l.get(i, ms[i]) for i in range(len(parts)))
        accs = tuple(al_l[i] * accs[i] + pv_l[i] if i in pv_l else accs[i] for i in range(len(parts)))
        return ms, accs

    carry = (tuple(jnp.full((sub, 1), NEG_BIG, F32) for _ in parts),
             tuple(jnp.zeros((sub, 2 * LANE), F32) for _ in parts))
    carry = lax.fori_loop(0, n_full, lambda kb, c: block(kb, None, c), carry)
    for d in range(tq // tk):
        carry = block(n_full + d, d, carry)
    accs = carry[1]
    num = [jnp.concatenate([accs[mp * nr + r][:, :LANE] for r in range(nr)], axis=0) for mp in range(2)]
    den = [jnp.concatenate([accs[mp * nr + r][:, LANE:] for r in range(nr)], axis=0) for mp in range(2)]
    o = num[0] / den[0] - lam_ref[...] * (num[1] / den[1])
    o_ref[...] = _rms_head(o, nw_ref[...], post_scale).astype(o_ref.dtype)


def _attn_prompt(hmain, lamv, nw, layer, *, batch, seq, n_heads, hd, post_scale, tq, tk, out_dtype):
    m = hmain.shape[0]
    nq = seq // tq
    sub = min(tk, LANE)
    assert tq % tk == 0 and tk % sub == 0
    vec = pl.BlockSpec((None, 1, LANE), lambda b, h, i: (layer, 0, 0))
    return pl.pallas_call(
        functools.partial(_attn_prompt_kernel, tq=tq, tk=tk, sub=sub, hd=hd, scale=hd ** -0.5,
                          post_scale=post_scale),
        grid=(batch, n_heads, nq),
        in_specs=[pl.BlockSpec((tq, LANE), lambda b, h, i: (b * nq + i, h)),
                  pl.BlockSpec((seq, LANE), lambda b, h, i: (b, n_heads + h)),
                  pl.BlockSpec((seq, LANE), lambda b, h, i: (b, 2 * n_heads + h)),
                  vec, vec],
        out_specs=pl.BlockSpec((tq, LANE), lambda b, h, i: (b * nq + i, h)),
        out_shape=jax.ShapeDtypeStruct((m, n_heads * LANE), out_dtype),
        scratch_shapes=[pltpu.VMEM((2, tq, LANE), BF16)],
        compiler_params=_cparams(("parallel", "parallel", "arbitrary"), 32),
        name="attn_prompt",
    )(hmain, hmain, hmain, lamv, nw)


def _attn_decode_kernel(pt_ref, q_ref, kn_ref, vn_ref, lam_ref, nw_ref, *rest,
                        pp, n_heads, hd, n_tok, page, scale, post_scale):
    k_refs, v_refs = rest[:pp], rest[pp:2 * pp]
    o_ref = rest[2 * pp]
    qbd_ref, m_ref, l_ref, acc_ref, kpad_ref, vpad_ref = rest[2 * pp + 1:]
    p_id = pl.program_id(1)
    rows = n_heads * SUBLANE
    width = n_heads * LANE

    def update(s_parts, v_parts):
        s = jnp.concatenate(s_parts, axis=1) if len(s_parts) > 1 else s_parts[0]
        m_old = m_ref[:, :1]
        mn = jnp.maximum(m_old, jnp.max(s, axis=-1, keepdims=True))
        p = jnp.exp(s - mn)
        al = jnp.exp(m_old - mn)
        l_ref[...] = jnp.broadcast_to(al * l_ref[:, :1] + jnp.sum(p, axis=-1, keepdims=True), (rows, LANE))
        m_ref[...] = jnp.broadcast_to(mn, (rows, LANE))
        pv = None
        for i, v in enumerate(v_parts):
            t = jnp.dot(p[:, i * page:(i + 1) * page].astype(BF16), v, preferred_element_type=F32)
            pv = t if pv is None else pv + t
        acc_ref[...] = al * acc_ref[...] + pv

    @pl.when(p_id == 0)
    def _():
        q = q_ref[...] * scale
        r8 = lax.broadcasted_iota(jnp.int32, q.shape, 0)
        q8 = jnp.where(r8 < n_tok, q, pltpu.roll(q, n_tok, 0))
        qt = jnp.concatenate([q8] * n_heads, axis=0)
        r = lax.broadcasted_iota(jnp.int32, qt.shape, 0)
        c = lax.broadcasted_iota(jnp.int32, qt.shape, 1)
        sel = (c >> _log2(hd)) == (r >> _log2(SUBLANE)) * 2 + ((r & (SUBLANE - 1)) >> _log2(n_tok))
        qbd_ref[...] = jnp.where(sel, qt, 0.0).astype(BF16)
        m_ref[...] = jnp.full(m_ref.shape, NEG_BIG, F32)
        l_ref[...] = jnp.zeros(l_ref.shape, F32)
        acc_ref[...] = jnp.zeros(acc_ref.shape, F32)
        kpad_ref[...] = jnp.zeros(kpad_ref.shape, F32)
        vpad_ref[...] = jnp.zeros(vpad_ref.shape, F32)
        kpad_ref[0:SUBLANE, :] = kn_ref[...]
        vpad_ref[0:SUBLANE, :] = vn_ref[...]
        s = _bdot_nt(qbd_ref[...], kpad_ref[...])
        rr = lax.broadcasted_iota(jnp.int32, s.shape, 0)
        cc = lax.broadcasted_iota(jnp.int32, s.shape, 1)
        s = jnp.where(cc <= (rr & (n_tok - 1)), s, NEG_BIG)
        update([s], [vpad_ref[...].astype(BF16)])

    qbd = qbd_ref[...]
    s_parts = [_bdot(qbd, k_refs[i][...]) for i in range(pp)]
    v_parts = [jnp.concatenate([v_refs[i][pl.ds(h, page, stride=n_heads), :] for h in range(n_heads)],
                               axis=1).astype(BF16) for i in range(pp)]
    update(s_parts, v_parts)

    @pl.when(p_id == pl.num_programs(1) - 1)
    def _():
        accn = acc_ref[...] / l_ref[:, :1]
        c = lax.broadcasted_iota(jnp.int32, (SUBLANE, width), 1)
        res = jnp.zeros((SUBLANE, width), F32)
        for h in range(n_heads):
            res = res + jnp.where((c >> _log2(LANE)) == h, accn[h * SUBLANE:(h + 1) * SUBLANE, :], 0.0)
        o8 = res - lam_ref[...][:, :1] * pltpu.roll(res, SUBLANE - n_tok, 0)
        nw = nw_ref[...]
        for h in range(n_heads):
            o_ref[:, h * LANE:(h + 1) * LANE] = _rms_head(o8[:, h * LANE:(h + 1) * LANE], nw, post_scale)


def _attn_decode(hmain, cache_kt, cache_v2, page_table, lamv, nw, layer, *, n_heads, hd, n_tok,
                 post_scale, pp):
    m = hmain.shape[0]
    batch, n_pages = page_table.shape
    page = cache_kt.shape[3]
    width = n_heads * LANE
    rows = n_heads * SUBLANE
    assert 2 * n_tok == SUBLANE and page == LANE and n_pages % pp == 0
    assert cache_kt.shape[2] == width and cache_v2.shape[2:] == (page * n_heads, LANE)
    vec = pl.BlockSpec((None, 1, LANE), lambda b, p, pt: (layer, 0, 0))

    def page_spec(i, shape):
        return pl.BlockSpec((None, None) + shape, lambda b, p, pt: (layer, pt[b, p * pp + i], 0, 0))

    in_specs = ([pl.BlockSpec((SUBLANE, width), lambda b, p, pt: (b, 0)),
                 pl.BlockSpec((SUBLANE, width), lambda b, p, pt: (b, 1)),
                 pl.BlockSpec((SUBLANE, width), lambda b, p, pt: (b, 2)),
                 vec, vec]
                + [page_spec(i, (width, page)) for i in range(pp)]
                + [page_spec(i, (page * n_heads, LANE)) for i in range(pp)])
    grid_spec = pltpu.PrefetchScalarGridSpec(
        num_scalar_prefetch=1,
        grid=(batch, n_pages // pp),
        in_specs=in_specs,
        out_specs=pl.BlockSpec((SUBLANE, width), lambda b, p, pt: (b, 0)),
        scratch_shapes=[pltpu.VMEM((rows, width), BF16),
                        pltpu.VMEM((rows, LANE), F32),
                        pltpu.VMEM((rows, LANE), F32),
                        pltpu.VMEM((rows, width), F32),
                        pltpu.VMEM((page, width), F32),
                        pltpu.VMEM((page, width), F32)])
    return pl.pallas_call(
        functools.partial(_attn_decode_kernel, pp=pp, n_heads=n_heads, hd=hd, n_tok=n_tok, page=page,
                          scale=hd ** -0.5, post_scale=post_scale),
        grid_spec=grid_spec,
        out_shape=jax.ShapeDtypeStruct((m, width), F32),
        compiler_params=_cparams(("parallel", "arbitrary"), 48),
        name="attn_decode",
    )(page_table, hmain, hmain, hmain, lamv, nw, *([cache_kt] * pp), *([cache_v2] * pp))


def _gdn_kernel(u_ref, z_ref, bg_ref, buf_ref, s0_ref, cw_ref, dn_ref, o_ref, s_out_ref,
                halo_ref, st_ref, *, cin, n_valid, n_heads, dk, dv, conv_w):
    c_id = pl.program_id(1)
    C = GDN_CHUNK
    width = n_heads * dk
    heads = range(n_heads)

    @pl.when(c_id == 0)
    def _():
        halo_ref[...] = buf_ref[...]
        st_ref[...] = s0_ref[...]

    u = u_ref[...]
    ext = jnp.concatenate([halo_ref[...], u], axis=0)
    cw = cw_ref[...]
    conv = None
    for i in range(conv_w):
        sh = SUBLANE - (conv_w - 1) + i
        term = ext[sh:sh + cin, :] * cw[i:i + 1, :]
        conv = term if conv is None else conv + term
    halo_ref[...] = u[cin - SUBLANE:cin, :]
    act = conv * _sigmoid(conv)
    bg = bg_ref[...]
    if n_valid < cin:
        valid = lax.broadcasted_iota(jnp.int32, (cin, 1), 0) < n_valid
        act = jnp.where(valid, act, 0.0)
        bg = jnp.where(valid, bg, 0.0)

    def pad_rows(a):
        if cin == C:
            return a
        return jnp.concatenate([a, jnp.zeros((C - cin, a.shape[1]), a.dtype)], axis=0)

    ri = lax.broadcasted_iota(jnp.int32, (C, C), 0)
    ci = lax.broadcasted_iota(jnp.int32, (C, C), 1)
    causal = ri >= ci
    strict = ri > ci
    tril = jnp.where(causal, 1.0, 0.0).astype(BF16)
    eye = jnp.where(ri == ci, 1.0, 0.0).astype(F32)
    n_levels = _log2(C)

    bgp = pad_rows(bg)
    g_col = _dot_exact_lhs(tril, bgp)
    g_row = g_col.T
    g_last = g_col[C - 1:C, :]
    eg_all = jnp.exp(g_col)
    ekd_all = jnp.exp(g_last - g_col)
    egl_all = jnp.exp(g_last)

    q_l, k_l, low_l, qk_l, rhs_l = [], [], [], [], []
    for h in heads:
        q = act[:, h * dk:(h + 1) * dk]
        k = act[:, width + h * dk:width + (h + 1) * dk]
        v = act[:, 2 * width + h * dv:2 * width + (h + 1) * dv]
        q = pad_rows(q * lax.rsqrt(jnp.sum(q * q, axis=-1, keepdims=True) + L2_EPS) * (dk ** -0.5))
        k = pad_rows(k * lax.rsqrt(jnp.sum(k * k, axis=-1, keepdims=True) + L2_EPS))
        v = pad_rows(v)
        beta = _lane_bcast(bgp, h)
        gc = _lane_bcast(g_col, n_heads + h)
        gr = jnp.broadcast_to(g_row[n_heads + h:n_heads + h + 1, :], (C, C))
        decay = jnp.exp(jnp.where(causal, gc - gr, NEG_BIG))
        kb = k * beta
        kq = _bdot_nt(jnp.concatenate([kb, q], axis=0), k)
        low_l.append(kq[:C] * jnp.where(strict, decay, 0.0))
        qk_l.append(kq[C:] * decay)
        rhs_l.append(jnp.concatenate([v * beta, kb * _lane_bcast(eg_all, n_heads + h)], axis=1))
        q_l.append(q)
        k_l.append(k)

    def level_mask(lvl):
        return ((ri >> (lvl + 1)) == (ci >> (lvl + 1))) & ((ri >> lvl) != (ci >> lvl))

    m0 = level_mask(0)
    t_l = [eye - jnp.where(m0, low, 0.0) for low in low_l]
    for lvl in range(1, n_levels):
        mk = level_mask(lvl)
        et_l = [_bdot(jnp.where(mk, low, 0.0), t) for low, t in zip(low_l, t_l)]
        t_l = [t - _bdot(t, et) for t, et in zip(t_l, et_l)]

    sol_l = [_bdot(t, rhs) for t, rhs in zip(t_l, rhs_l)]
    z = z_ref[...]
    dn = dn_ref[...]
    for h in heads:
        val, kcd = sol_l[h][:, :dv], sol_l[h][:, dv:]
        st = st_ref[h]
        q_dec = q_l[h] * _lane_bcast(eg_all, n_heads + h)
        ks = _bdot(jnp.concatenate([kcd, q_dec], axis=0), st)
        v_new = val - ks[:C]
        o = ks[C:] + _bdot(qk_l[h], v_new)
        k_dec = k_l[h] * _lane_bcast(ekd_all, n_heads + h)
        st_ref[h] = st * egl_all[:, n_heads + h:n_heads + h + 1] + _bdot(k_dec.T, v_new)
        o = o[:cin, :]
        zh = z[:, h * dv:(h + 1) * dv]
        o = o * lax.rsqrt(jnp.mean(o * o, axis=-1, keepdims=True) + RMS_EPS) * dn * (zh * _sigmoid(zh))
        o_ref[:, h * dv:(h + 1) * dv] = o.astype(o_ref.dtype)

    @pl.when(c_id == pl.num_programs(1) - 1)
    def _():
        s_out_ref[...] = st_ref[...]


def _gdn(hmain, bg, buf8, s0, cw8, dn, layer, *, batch, rows_per_seq, n_valid, n_heads, dk, dv,
         conv_w, off_conv, off_z, out_dtype):
    m = hmain.shape[0]
    width = n_heads * dk
    cin = min(rows_per_seq, GDN_CHUNK)
    n_chunks = rows_per_seq // cin
    assert off_conv % (3 * width) == 0 and off_z % width == 0 and dk == dv == LANE == GDN_CHUNK
    return pl.pallas_call(
        functools.partial(_gdn_kernel, cin=cin, n_valid=n_valid, n_heads=n_heads, dk=dk, dv=dv,
                          conv_w=conv_w),
        grid=(batch, n_chunks),
        in_specs=[pl.BlockSpec((cin, 3 * width), lambda b, c: (b * n_chunks + c, off_conv // (3 * width))),
                  pl.BlockSpec((cin, width), lambda b, c: (b * n_chunks + c, off_z // width)),
                  pl.BlockSpec((cin, LANE), lambda b, c: (b * n_chunks + c, 0)),
                  pl.BlockSpec((None, SUBLANE, 3 * width), lambda b, c: (b, 0, 0)),
                  pl.BlockSpec((None, n_heads, dk, dv), lambda b, c: (b, 0, 0, 0)),
                  pl.BlockSpec((None, SUBLANE, 3 * width), lambda b, c: (layer, 0, 0)),
                  pl.BlockSpec((None, 1, dv), lambda b, c: (layer, 0, 0))],
        out_specs=[pl.BlockSpec((cin, width), lambda b, c: (b * n_chunks + c, 0)),
                   pl.BlockSpec((None, n_heads, dk, dv), lambda b, c: (b, 0, 0, 0))],
        out_shape=[jax.ShapeDtypeStruct((m, width), out_dtype),
                   jax.ShapeDtypeStruct((batch, n_heads, dk, dv), F32)],
        scratch_shapes=[pltpu.VMEM((SUBLANE, 3 * width), F32),
                        pltpu.VMEM((n_heads, dk, dv), F32)],
        compiler_params=_cparams(("parallel", "arbitrary"), 48),
        name="gdn",
    )(hmain, hmain, bg, buf8, s0, cw8, dn)


def _layernorm(y, g_ref, b_ref):
    mu = jnp.mean(y, axis=-1, keepdims=True)
    yc = y - mu
    var = jnp.mean(yc * yc, axis=-1, keepdims=True)
    return yc * lax.rsqrt(var + LN_EPS) * g_ref[...] + b_ref[...]


def _merge_proj_ln_kernel(oa_ref, od_ref, wa_ref, wb_ref, ga_ref, gb_ref, wo_ref, r_ref, g_ref, b_ref,
                          o_ref, *, alpha):
    j = pl.program_id(1)

    @pl.when(j == 0)
    def _():
        o_ref[...] = alpha * r_ref[...]

    mixed = ga_ref[...] * _bdot(oa_ref[...], wa_ref[...]) + gb_ref[...] * _bdot(od_ref[...], wb_ref[...])
    o_ref[...] += _bdot(mixed, wo_ref[...])

    @pl.when(j == pl.num_programs(1) - 1)
    def _():
        o_ref[...] = _layernorm(o_ref[...], g_ref, b_ref)


def _merge_proj_ln(oa, od, w_ba, w_bb, sg, w_out, resid, ln_g, ln_b, layer, *, alpha, tm, tn):
    m, wa = oa.shape
    wb = od.shape[1]
    d = w_out.shape[2]
    nj = d // tn
    vec = pl.BlockSpec((None, 1, d), lambda i, j: (layer, 0, 0))
    row = pl.BlockSpec((tm, d), lambda i, j: (i, 0))
    return pl.pallas_call(
        functools.partial(_merge_proj_ln_kernel, alpha=alpha),
        grid=(m // tm, nj),
        in_specs=[pl.BlockSpec((tm, wa), lambda i, j: (i, 0)),
                  pl.BlockSpec((tm, wb), lambda i, j: (i, 0)),
                  pl.BlockSpec((None, wa, tn), lambda i, j: (layer, 0, j)),
                  pl.BlockSpec((None, wb, tn), lambda i, j: (layer, 0, j)),
                  pl.BlockSpec((tm, tn), lambda i, j: (i, j)),
                  pl.BlockSpec((tm, tn), lambda i, j: (i, nj + j)),
                  pl.BlockSpec((None, tn, d), lambda i, j: (layer, j, 0)),
                  pl.BlockSpec((tm, d), lambda i, j: (i, 0), pipeline_mode=pl.Buffered(1)),
                  vec, vec],
        out_specs=row,
        out_shape=jax.ShapeDtypeStruct((m, d), F32),
        compiler_params=_cparams(("parallel", "arbitrary"), 56),
        name="merge_proj_ln",
    )(oa, od, w_ba, w_bb, sg, sg, w_out, resid, ln_g, ln_b)


def _mlp_ln_kernel(x_ref, wu_ref, wd_ref, g_ref, b_ref, o_ref, xb_ref, *, alpha):
    f = pl.program_id(1)

    @pl.when(f == 0)
    def _():
        x = x_ref[...]
        xb_ref[...] = x.astype(BF16)
        o_ref[...] = alpha * x

    hid = jnp.maximum(jnp.dot(xb_ref[...], wu_ref[...].astype(BF16), preferred_element_type=F32), 0.0)
    o_ref[...] += _bdot(hid * hid, wd_ref[...])

    @pl.when(f == pl.num_programs(1) - 1)
    def _():
        o_ref[...] = _layernorm(o_ref[...], g_ref, b_ref)


def _mlp_ln(x, w_up, w_down, ln_g, ln_b, layer, *, alpha, tm, tf):
    m, d = x.shape
    dff = w_up.shape[2]
    vec = pl.BlockSpec((None, 1, d), lambda i, f: (layer, 0, 0))
    row = pl.BlockSpec((tm, d), lambda i, f: (i, 0))
    return pl.pallas_call(
        functools.partial(_mlp_ln_kernel, alpha=alpha),
        grid=(m // tm, dff // tf),
        in_specs=[row,
                  pl.BlockSpec((None, d, tf), lambda i, f: (layer, 0, f)),
                  pl.BlockSpec((None, tf, d), lambda i, f: (layer, f, 0)),
                  vec, vec],
        out_specs=row,
        out_shape=jax.ShapeDtypeStruct((m, d), F32),
        scratch_shapes=[pltpu.VMEM((tm, d), BF16)],
        compiler_params=_cparams(("parallel", "arbitrary"), 58),
        name="mlp_ln",
    )(x, w_up, w_down, ln_g, ln_b)


def _rope_tables(pos, hd, rot):
    half = rot // 2
    inv_freq = ROPE_THETA ** (-jnp.arange(half, dtype=F32) * 2.0 / rot)
    ang = pos.astype(F32)[:, None] * inv_freq[None, :]
    cos, sin = jnp.cos(ang), jnp.sin(ang)
    lane = jnp.arange(LANE)
    within = lane % hd
    idx = within % half
    cos_l, sin_l = cos[:, idx], sin[:, idx]
    c = jnp.where(within < rot, cos_l, 1.0)
    sa = jnp.where(within < half, -sin_l, 0.0)
    sb = jnp.where((within >= half) & (within < rot), sin_l, 0.0)
    return c, sa, sb


def _lane_vec(v):
    return jnp.broadcast_to(v.astype(F32)[:, None, None], (v.shape[0], 1, LANE))


def kernel(x_prompt, x_sample, cache_k, cache_v, state_ssm, state_conv, page_table, w_in, conv_w, a_log, dt_bias, delta_norm_w, lambda_q1, lambda_k1, lambda_q2, lambda_k2, attn_norm_w, w_branch_a, w_branch_b, w_out, ln1_g, ln1_b, w_mlp_up, w_mlp_down, ln2_g, ln2_b):
    depth, d_model, in_cols = w_in.shape
    bp, sp, _ = x_prompt.shape
    bs, ss, _ = x_sample.shape
    _, n_pool, page, ha, _, hd = cache_k.shape
    _, _, hdn, dk, dv = state_ssm.shape
    dvh = cache_v.shape[-1]
    cw_len = conv_w.shape[1]
    n_pages = page_table.shape[1]
    past_len = n_pages * page
    qk_w = ha * 2 * hd
    wa = ha * dvh
    conv_ch = conv_w.shape[2]
    wdv = hdn * dv
    rot = hd // 4
    off_k, off_v, off_conv = qk_w, 2 * qk_w, 2 * qk_w + wa
    off_z = off_conv + conv_ch
    n_main = off_z + wdv
    off_gate = n_main + 2 * hdn
    assert in_cols == off_gate + 2 * d_model and 2 * hd == LANE and dvh == LANE
    assert qk_w == wa == wdv and conv_ch == 3 * wdv and cw_len <= SUBLANE
    alpha = (2 * depth) ** 0.25
    rows_s = SUBLANE

    w_t = jnp.swapaxes(w_in, 1, 2)
    cache_kt = jnp.transpose(cache_k, (0, 1, 3, 4, 5, 2)).reshape(depth, n_pool, qk_w, page)
    cache_v2 = cache_v.reshape(depth, n_pool, page * ha, dvh)

    alog = jnp.pad(a_log.astype(F32), ((0, 0), (hdn, LANE - 2 * hdn)))[:, None, :]
    dtb = jnp.pad(dt_bias.astype(F32), ((0, 0), (hdn, LANE - 2 * hdn)))[:, None, :]
    cw8 = jnp.pad(conv_w.astype(F32), ((0, 0), (0, SUBLANE - cw_len), (0, 0)))
    dn = delta_norm_w.astype(F32)[:, None, :]
    nw = attn_norm_w.astype(F32)[:, None, :]
    vec3 = lambda a: a.astype(F32)[:, None, :]
    g1, b1, g2, b2 = vec3(ln1_g), vec3(ln1_b), vec3(ln2_g), vec3(ln2_b)
    lam_init = jnp.asarray([0.8 - 0.6 * math.exp(-0.3 * l) for l in range(depth)], F32)
    lam = (jnp.exp(jnp.sum(lambda_q1.astype(F32) * lambda_k1.astype(F32), axis=-1))
           - jnp.exp(jnp.sum(lambda_q2.astype(F32) * lambda_k2.astype(F32), axis=-1)) + lam_init)
    lamv = _lane_vec(lam)

    tabs_p = _rope_tables(jnp.tile(jnp.arange(sp), bp), hd, rot)
    tabs_s = _rope_tables(jnp.tile(past_len + jnp.arange(rows_s), bs), hd, rot)

    mp = bp * sp
    ms = bs * rows_s
    xs_pad = jnp.pad(x_sample, ((0, 0), (0, rows_s - ss), (0, 0))).reshape(ms, d_model)
    xp = x_prompt.reshape(mp, d_model)
    state = {"p": xp, "s": xs_pad}
    buf_p = jnp.zeros((bp, SUBLANE, conv_ch), F32)
    s0_p = jnp.zeros((bp, hdn, dk, dv), F32)

    tn_main = _pick(math.gcd(qk_w, n_main), (1024, 512, 256, 128))
    tn_d = _pick(math.gcd(d_model, n_main), (512, 256, 128))
    tn_mix = _pick(d_model, (256, 128))
    tf = _pick(w_mlp_up.shape[2], (512, 256, 128))
    tq = _pick(sp, (1024, 512, 256, 128))
    tk_attn = min(tq, 512)
    pp = _pick(n_pages, (8, 4, 2, 1))
    tm_of = {"p": _pick(mp, (1024, 512, 256, 128)), "s": ms}

    outs = {"p": ([], [], [], []), "s": ([], [], [], [])}
    for l in range(depth):
        post = 1.0 - (0.8 - 0.6 * math.exp(-0.3 * l))
        for grp in ("p", "s"):
            xb = state[grp]
            tm = tm_of[grp]
            tabs = tabs_p if grp == "p" else tabs_s
            hmain = _inproj(xb, w_t, l, tabs, n_main=n_main, n_rope=off_v, half=rot // 2,
                            tm=tm, tn=tn_main)
            sg, bg = _gates(xb, w_t, l, alog, dtb, off_small=n_main, n_heads=hdn, n_out=2 * d_model,
                            tm=tm, tn=tn_d)
            if grp == "p":
                oa = _attn_prompt(hmain, lamv, nw, l, batch=bp, seq=sp, n_heads=ha, hd=hd,
                                  post_scale=post, tq=tq, tk=tk_attn, out_dtype=BF16)
                od, ssm = _gdn(hmain, bg, buf_p, s0_p, cw8, dn, l, batch=bp, rows_per_seq=sp,
                               n_valid=sp, n_heads=hdn, dk=dk, dv=dv, conv_w=cw_len,
                               off_conv=off_conv, off_z=off_z, out_dtype=BF16)
                h3 = hmain.reshape(bp, sp, n_main)
                k_new = h3[:, :, off_k:off_v]
                v_new = h3[:, :, off_v:off_conv]
                conv_new = h3[:, sp - (cw_len - 1):, off_conv:off_z]
            else:
                oa = _attn_decode(hmain, cache_kt, cache_v2, page_table, lamv, nw, l, n_heads=ha,
                                  hd=hd, n_tok=ss, post_scale=post, pp=pp)
                buf_s = jnp.pad(state_conv[l].astype(F32),
                                ((0, 0), (SUBLANE - (cw_len - 1), 0), (0, 0)))
                od, ssm = _gdn(hmain, bg, buf_s, state_ssm[l].astype(F32), cw8, dn, l, batch=bs,
                               rows_per_seq=rows_s, n_valid=ss, n_heads=hdn, dk=dk, dv=dv,
                               conv_w=cw_len, off_conv=off_conv, off_z=off_z, out_dtype=F32)
                h3 = hmain.reshape(bs, rows_s, n_main)
                k_new = h3[:, :ss, off_k:off_v]
                v_new = h3[:, :ss, off_v:off_conv]
                conv_new = h3[:, ss - (cw_len - 1):ss, off_conv:off_z]
            x1 = _merge_proj_ln(oa, od, w_branch_a, w_branch_b, sg, w_out, xb, g1, b1, l, alpha=alpha,
                                tm=tm, tn=tn_mix)
            state[grp] = _mlp_ln(x1, w_mlp_up, w_mlp_down, g2, b2, l, alpha=alpha, tm=tm, tf=tf)
            ko, vo, so, co = outs[grp]
            nb = k_new.shape[0]
            ko.append(k_new.reshape(nb, -1, ha, 2, hd))
            vo.append(v_new.reshape(nb, -1, ha, dvh))
            so.append(ssm)
            co.append(conv_new)

    y_p = state["p"].reshape(bp, sp, d_model)
    y_s = state["s"].reshape(bs, rows_s, d_model)[:, :ss]
    kp, vp, sp_l, cp = (jnp.stack(t) for t in outs["p"])
    ks, vs, ss_l, cs = (jnp.stack(t) for t in outs["s"])
    return (y_p, y_s, kp, vp, sp_l, cp, ks, vs, ss_l, cs)
```

```python
import functools
import math

import jax
import jax.numpy as jnp
from jax import lax
from jax.experimental import pallas as pl
from jax.experimental.pallas import tpu as pltpu

F32 = jnp.float32
BF16 = jnp.bfloat16
LANE = 128
SUBLANE = 8
MIB = 1024 * 1024

ROPE_THETA = 500000.0
LN_EPS = 1e-5
RMS_EPS = 1e-6
L2_EPS = 1e-6
GDN_CHUNK = 128
NEG_BIG = -1e30


def _cparams(sem, vmem_mib):
    return pltpu.CompilerParams(dimension_semantics=sem, vmem_limit_bytes=vmem_mib * MIB)


def _pick(n, cands):
    for c in cands:
        if n % c == 0:
            return c
    raise ValueError(f"no tile for {n} in {cands}")


def _log2(n):
    assert n & (n - 1) == 0
    return n.bit_length() - 1


def _sigmoid(x):
    return 1.0 / (1.0 + jnp.exp(-x))


def _bdot(a, b):
    return jnp.dot(a.astype(BF16), b.astype(BF16), preferred_element_type=F32)


def _bdot_nt(a, b):
    return lax.dot_general(a.astype(BF16), b.astype(BF16), (((1,), (1,)), ((), ())),
                           preferred_element_type=F32)


def _split3(a):
    a1 = a.astype(BF16)
    r1 = a - a1.astype(F32)
    a2 = r1.astype(BF16)
    a3 = (r1 - a2.astype(F32)).astype(BF16)
    return a1, a2, a3


def _dot_exact_lhs(mask_bf16, b):
    d = functools.partial(jnp.dot, preferred_element_type=F32)
    b1, b2, b3 = _split3(b)
    return d(mask_bf16, b1) + (d(mask_bf16, b2) + d(mask_bf16, b3))


def _lane_bcast(x, j):
    return jnp.broadcast_to(x[:, j:j + 1], x.shape)


def _inproj_kernel(x_ref, w_ref, c_ref, sa_ref, sb_ref, o_ref, xb_ref, *, n_rope_blocks, tn, half):
    j = pl.program_id(1)

    @pl.when(j == 0)
    def _():
        xb_ref[...] = x_ref[...].astype(BF16)

    acc = _bdot_nt(xb_ref[...], w_ref[...])

    @pl.when(j >= n_rope_blocks)
    def _():
        o_ref[...] = acc

    @pl.when(j < n_rope_blocks)
    def _():
        c, sa, sb = c_ref[...], sa_ref[...], sb_ref[...]
        for g in range(tn // LANE):
            xg = acc[:, g * LANE:(g + 1) * LANE]
            o_ref[:, g * LANE:(g + 1) * LANE] = (
                xg * c + pltpu.roll(xg, LANE - half, 1) * sa + pltpu.roll(xg, half, 1) * sb)


def _inproj(x, w_t, layer, tabs, *, n_main, n_rope, half, tm, tn):
    m, d = x.shape
    c, sa, sb = tabs
    tab_spec = pl.BlockSpec((tm, LANE), lambda i, j: (i, 0))
    return pl.pallas_call(
        functools.partial(_inproj_kernel, n_rope_blocks=n_rope // tn, tn=tn, half=half),
        grid=(m // tm, n_main // tn),
        in_specs=[pl.BlockSpec((tm, d), lambda i, j: (i, 0)),
                  pl.BlockSpec((None, tn, d), lambda i, j: (layer, j, 0)),
                  tab_spec, tab_spec, tab_spec],
        out_specs=pl.BlockSpec((tm, tn), lambda i, j: (i, j)),
        out_shape=jax.ShapeDtypeStruct((m, n_main), F32),
        scratch_shapes=[pltpu.VMEM((tm, d), BF16)],
        compiler_params=_cparams(("parallel", "arbitrary"), 56),
        name="inproj",
    )(x, w_t, c, sa, sb)


def _gates_kernel(x_ref, wa_ref, wb_ref, alog_ref, dtb_ref, o_ref, bg_ref, xb_ref, *, skip, n_heads):
    j = pl.program_id(1)

    @pl.when(j == 0)
    def _():
        xb_ref[...] = x_ref[...].astype(BF16)
        w = wa_ref[0:skip, :]
        w = jnp.concatenate([w, jnp.zeros((LANE - skip, w.shape[1]), w.dtype)], axis=0)
        h = _bdot_nt(xb_ref[...], w)
        lane = lax.broadcasted_iota(jnp.int32, h.shape, 1)
        beta = _sigmoid(h)
        t = h + dtb_ref[...]
        softplus = jnp.maximum(t, 0.0) + jnp.log1p(jnp.exp(-jnp.abs(t)))
        g = -jnp.exp(alog_ref[...]) * softplus
        bg_ref[...] = jnp.where(lane < n_heads, beta, jnp.where(lane < 2 * n_heads, g, 0.0))

    w = jnp.concatenate([wa_ref[skip:, :].astype(BF16), wb_ref[...].astype(BF16)], axis=0)
    o_ref[...] = _sigmoid(_bdot_nt(xb_ref[...], w))


def _gates(x, w_t, layer, alog, dtb, *, off_small, n_heads, n_out, tm, tn):
    m, d = x.shape
    skip = 2 * n_heads
    assert off_small % tn == 0 and tn % skip == 0 and skip % SUBLANE == 0 and skip <= LANE
    vec = pl.BlockSpec((None, 1, LANE), lambda i, j: (layer, 0, 0))
    return pl.pallas_call(
        functools.partial(_gates_kernel, skip=skip, n_heads=n_heads),
        grid=(m // tm, n_out // tn),
        in_specs=[pl.BlockSpec((tm, d), lambda i, j: (i, 0)),
                  pl.BlockSpec((None, tn, d), lambda i, j: (layer, off_small // tn + j, 0)),
                  pl.BlockSpec((None, skip, d),
                               lambda i, j: (layer, (off_small + (j + 1) * tn) // skip, 0)),
                  vec, vec],
        out_specs=[pl.BlockSpec((tm, tn), lambda i, j: (i, j)),
                   pl.BlockSpec((tm, LANE), lambda i, j: (i, 0))],
        out_shape=[jax.ShapeDtypeStruct((m, n_out), F32), jax.ShapeDtypeStruct((m, LANE), F32)],
        scratch_shapes=[pltpu.VMEM((tm, d), BF16)],
        compiler_params=_cparams(("parallel", "arbitrary"), 56),
        name="gates",
    )(x, w_t, w_t, alog, dtb)


def _rms_head(o, nw, post_scale):
    return o * lax.rsqrt(jnp.mean(o * o, axis=-1, keepdims=True) + RMS_EPS) * nw * post_scale


def _attn_prompt_kernel(q_ref, k_ref, v_ref, lam_ref, nw_ref, o_ref, qm_ref,
                        *, tq, tk, sub, hd, scale, post_scale):
    qi = pl.program_id(2)
    q = q_ref[...] * scale
    lane = lax.broadcasted_iota(jnp.int32, q.shape, 1)
    qm_ref[0] = jnp.where(lane < hd, q, 0.0).astype(BF16)
    qm_ref[1] = jnp.where(lane >= hd, q, 0.0).astype(BF16)
    row = lax.broadcasted_iota(jnp.int32, (sub, tk), 0)
    col = lax.broadcasted_iota(jnp.int32, (sub, tk), 1)
    nr = tq // sub
    parts = [(mp, r) for mp in range(2) for r in range(nr)]
    ones = jnp.ones((tk, LANE), BF16)
    n_full = (qi * tq) // tk

    def block(kb, d, carry):
        ms, accs = carry
        start = pl.multiple_of(kb * tk, tk)
        k = k_ref[pl.ds(start, tk), :].astype(BF16)
        v1 = jnp.concatenate([v_ref[pl.ds(start, tk), :].astype(BF16), ones], axis=1)
        if d is None:
            live = list(range(len(parts)))
            masked = set()
        else:
            live = [i for i, (_, r) in enumerate(parts) if (r + 1) * sub - 1 >= d * tk]
            masked = {i for i in live if parts[i][1] * sub < (d + 1) * tk - 1}
        s_l = {i: _bdot_nt(qm_ref[parts[i][0], pl.ds(parts[i][1] * sub, sub), :], k) for i in live}
        for i in masked:
            s_l[i] = jnp.where(col + d * tk <= row + parts[i][1] * sub, s_l[i], NEG_BIG)
        mn_l = {i: jnp.maximum(ms[i], jnp.max(s_l[i], axis=-1, keepdims=True)) for i in live}
        p_l = {i: jnp.exp(s_l[i] - mn_l[i]).astype(BF16) for i in live}
        al_l = {i: jnp.exp(ms[i] - mn_l[i]) for i in live}
        pv_l = {i: jnp.dot(p_l[i], v1, preferred_element_type=F32) for i in live}
        ms = tuple(mn_l.get(i, ms[i]) for i in range(len(parts)))
        accs = tuple(al_l[i] * accs[i] + pv_l[i] if i in pv_l else accs[i] for i in range(len(parts)))
        return ms, accs

    carry = (tuple(jnp.full((sub, 1), NEG_BIG, F32) for _ in parts),
             tuple(jnp.zeros((sub, 2 * LANE), F32) for _ in parts))
    carry = lax.fori_loop(0, n_full, lambda kb, c: block(kb, None, c), carry)
    for d in range(tq // tk):
        carry = block(n_full + d, d, carry)
    accs = carry[1]
    num = [jnp.concatenate([accs[mp * nr + r][:, :LANE] for r in range(nr)], axis=0) for mp in range(2)]
    den = [jnp.concatenate([accs[mp * nr + r][:, LANE:] for r in range(nr)], axis=0) for mp in range(2)]
    o = num[0] / den[0] - lam_ref[...] * (num[1] / den[1])
    o_ref[...] = _rms_head(o, nw_ref[...], post_scale).astype(o_ref.dtype)


def _attn_prompt(hmain, lamv, nw, layer, *, batch, seq, n_heads, hd, post_scale, tq, tk, out_dtype):
    m = hmain.shape[0]
    nq = seq // tq
    sub = min(tk, LANE)
    assert tq % tk == 0 and tk % sub == 0
    vec = pl.BlockSpec((None, 1, LANE), lambda b, h, i: (layer, 0, 0))
    return pl.pallas_call(
        functools.partial(_attn_prompt_kernel, tq=tq, tk=tk, sub=sub, hd=hd, scale=hd ** -0.5,
                          post_scale=post_scale),
        grid=(batch, n_heads, nq),
        in_specs=[pl.BlockSpec((tq, LANE), lambda b, h, i: (b * nq + i, h)),
                  pl.BlockSpec((seq, LANE), lambda b, h, i: (b, n_heads + h)),
                  pl.BlockSpec((seq, LANE), lambda b, h, i: (b, 2 * n_heads + h)),
                  vec, vec],
        out_specs=pl.BlockSpec((tq, LANE), lambda b, h, i: (b * nq + i, h)),
        out_shape=jax.ShapeDtypeStruct((m, n_heads * LANE), out_dtype),
        scratch_shapes=[pltpu.VMEM((2, tq, LANE), BF16)],
        compiler_params=_cparams(("parallel", "parallel", "arbitrary"), 32),
        name="attn_prompt",
    )(hmain, hmain, hmain, lamv, nw)


def _attn_decode_kernel(pt_ref, q_ref, kn_ref, vn_ref, lam_ref, nw_ref, *rest,
                        pp, n_heads, hd, n_tok, page, scale, post_scale):
    k_refs, v_refs = rest[:pp], rest[pp:2 * pp]
    o_ref = rest[2 * pp]
    qbd_ref, m_ref, l_ref, acc_ref, kpad_ref, vpad_ref = rest[2 * pp + 1:]
    p_id = pl.program_id(1)
    rows = n_heads * SUBLANE
    width = n_heads * LANE

    def update(s_parts, v_parts):
        s = jnp.concatenate(s_parts, axis=1) if len(s_parts) > 1 else s_parts[0]
        m_old = m_ref[:, :1]
        mn = jnp.maximum(m_old, jnp.max(s, axis=-1, keepdims=True))
        p = jnp.exp(s - mn)
        al = jnp.exp(m_old - mn)
        l_ref[...] = jnp.broadcast_to(al * l_ref[:, :1] + jnp.sum(p, axis=-1, keepdims=True), (rows, LANE))
        m_ref[...] = jnp.broadcast_to(mn, (rows, LANE))
        pv = None
        for i, v in enumerate(v_parts):
            t = jnp.dot(p[:, i * page:(i + 1) * page].astype(BF16), v, preferred_element_type=F32)
            pv = t if pv is None else pv + t
        acc_ref[...] = al * acc_ref[...] + pv

    @pl.when(p_id == 0)
    def _():
        q = q_ref[...] * scale
        r8 = lax.broadcasted_iota(jnp.int32, q.shape, 0)
        q8 = jnp.where(r8 < n_tok, q, pltpu.roll(q, n_tok, 0))
        qt = jnp.concatenate([q8] * n_heads, axis=0)
        r = lax.broadcasted_iota(jnp.int32, qt.shape, 0)
        c = lax.broadcasted_iota(jnp.int32, qt.shape, 1)
        sel = (c >> _log2(hd)) == (r >> _log2(SUBLANE)) * 2 + ((r & (SUBLANE - 1)) >> _log2(n_tok))
        qbd_ref[...] = jnp.where(sel, qt, 0.0).astype(BF16)
        m_ref[...] = jnp.full(m_ref.shape, NEG_BIG, F32)
        l_ref[...] = jnp.zeros(l_ref.shape, F32)
        acc_ref[...] = jnp.zeros(acc_ref.shape, F32)
        kpad_ref[...] = jnp.zeros(kpad_ref.shape, F32)
        vpad_ref[...] = jnp.zeros(vpad_ref.shape, F32)
        kpad_ref[0:SUBLANE, :] = kn_ref[...]
        vpad_ref[0:SUBLANE, :] = vn_ref[...]
        s = _bdot_nt(qbd_ref[...], kpad_ref[...])
        rr = lax.broadcasted_iota(jnp.int32, s.shape, 0)
        cc = lax.broadcasted_iota(jnp.int32, s.shape, 1)
        s = jnp.where(cc <= (rr & (n_tok - 1)), s, NEG_BIG)
        update([s], [vpad_ref[...].astype(BF16)])

    qbd = qbd_ref[...]
    s_parts = [_bdot(qbd, k_refs[i][...]) for i in range(pp)]
    v_parts = [jnp.concatenate([v_refs[i][pl.ds(h, page, stride=n_heads), :] for h in range(n_heads)],
                               axis=1).astype(BF16) for i in range(pp)]
    update(s_parts, v_parts)

    @pl.when(p_id == pl.num_programs(1) - 1)
    def _():
        accn = acc_ref[...] / l_ref[:, :1]
        c = lax.broadcasted_iota(jnp.int32, (SUBLANE, width), 1)
        res = jnp.zeros((SUBLANE, width), F32)
        for h in range(n_heads):
            res = res + jnp.where((c >> _log2(LANE)) == h, accn[h * SUBLANE:(h + 1) * SUBLANE, :], 0.0)
        o8 = res - lam_ref[...][:, :1] * pltpu.roll(res, SUBLANE - n_tok, 0)
        nw = nw_ref[...]
        for h in range(n_heads):
            o_ref[:, h * LANE:(h + 1) * LANE] = _rms_head(o8[:, h * LANE:(h + 1) * LANE], nw, post_scale)


def _attn_decode(hmain, cache_kt, cache_v2, page_table, lamv, nw, layer, *, n_heads, hd, n_tok,
                 post_scale, pp):
    m = hmain.shape[0]
    batch, n_pages = page_table.shape
    page = cache_kt.shape[3]
    width = n_heads * LANE
    rows = n_heads * SUBLANE
    assert 2 * n_tok == SUBLANE and page == LANE and n_pages % pp == 0
    assert cache_kt.shape[2] == width and cache_v2.shape[2:] == (page * n_heads, LANE)
    vec = pl.BlockSpec((None, 1, LANE), lambda b, p, pt: (layer, 0, 0))

    def page_spec(i, shape):
        return pl.BlockSpec((None, None) + shape, lambda b, p, pt: (layer, pt[b, p * pp + i], 0, 0))

    in_specs = ([pl.BlockSpec((SUBLANE, width), lambda b, p, pt: (b, 0)),
                 pl.BlockSpec((SUBLANE, width), lambda b, p, pt: (b, 1)),
                 pl.BlockSpec((SUBLANE, width), lambda b, p, pt: (b, 2)),
                 vec, vec]
                + [page_spec(i, (width, page)) for i in range(pp)]
                + [page_spec(i, (page * n_heads, LANE)) for i in range(pp)])
    grid_spec = pltpu.PrefetchScalarGridSpec(
        num_scalar_prefetch=1,
        grid=(batch, n_pages // pp),
        in_specs=in_specs,
        out_specs=pl.BlockSpec((SUBLANE, width), lambda b, p, pt: (b, 0)),
        scratch_shapes=[pltpu.VMEM((rows, width), BF16),
                        pltpu.VMEM((rows, LANE), F32),
                        pltpu.VMEM((rows, LANE), F32),
                        pltpu.VMEM((rows, width), F32),
                        pltpu.VMEM((page, width), F32),
                        pltpu.VMEM((page, width), F32)])
    return pl.pallas_call(
        functools.partial(_attn_decode_kernel, pp=pp, n_heads=n_heads, hd=hd, n_tok=n_tok, page=page,
                          scale=hd ** -0.5, post_scale=post_scale),
        grid_spec=grid_spec,
        out_shape=jax.ShapeDtypeStruct((m, width), F32),
        compiler_params=_cparams(("parallel", "arbitrary"), 56),
        name="attn_decode",
    )(page_table, hmain, hmain, hmain, lamv, nw, *([cache_kt] * pp), *([cache_v2] * pp))


def _gdn_kernel(u_ref, z_ref, bg_ref, buf_ref, s0_ref, cw_ref, dn_ref, o_ref, s_out_ref,
                halo_ref, st_ref, *, cin, n_valid, n_heads, dk, dv, conv_w):
    c_id = pl.program_id(1)
    C = GDN_CHUNK
    width = n_heads * dk
    heads = range(n_heads)

    @pl.when(c_id == 0)
    def _():
        halo_ref[...] = buf_ref[...]
        st_ref[...] = s0_ref[...]

    u = u_ref[...]
    ext = jnp.concatenate([halo_ref[...], u], axis=0)
    cw = cw_ref[...]
    conv = None
    for i in range(conv_w):
        sh = SUBLANE - (conv_w - 1) + i
        term = ext[sh:sh + cin, :] * cw[i:i + 1, :]
        conv = term if conv is None else conv + term
    halo_ref[...] = u[cin - SUBLANE:cin, :]
    act = conv * _sigmoid(conv)
    bg = bg_ref[...]
    if n_valid < cin:
        valid = lax.broadcasted_iota(jnp.int32, (cin, 1), 0) < n_valid
        act = jnp.where(valid, act, 0.0)
        bg = jnp.where(valid, bg, 0.0)

    def pad_rows(a):
        if cin == C:
            return a
        return jnp.concatenate([a, jnp.zeros((C - cin, a.shape[1]), a.dtype)], axis=0)

    ri = lax.broadcasted_iota(jnp.int32, (C, C), 0)
    ci = lax.broadcasted_iota(jnp.int32, (C, C), 1)
    causal = ri >= ci
    strict = ri > ci
    tril = jnp.where(causal, 1.0, 0.0).astype(BF16)
    eye = jnp.where(ri == ci, 1.0, 0.0).astype(F32)
    n_levels = _log2(C)

    bgp = pad_rows(bg)
    g_col = _dot_exact_lhs(tril, bgp)
    g_row = g_col.T
    g_last = g_col[C - 1:C, :]
    eg_all = jnp.exp(g_col)
    ekd_all = jnp.exp(g_last - g_col)
    egl_all = jnp.exp(g_last)

    q_l, k_l, low_l, qk_l, rhs_l = [], [], [], [], []
    for h in heads:
        q = act[:, h * dk:(h + 1) * dk]
        k = act[:, width + h * dk:width + (h + 1) * dk]
        v = act[:, 2 * width + h * dv:2 * width + (h + 1) * dv]
        q = pad_rows(q * lax.rsqrt(jnp.sum(q * q, axis=-1, keepdims=True) + L2_EPS) * (dk ** -0.5))
        k = pad_rows(k * lax.rsqrt(jnp.sum(k * k, axis=-1, keepdims=True) + L2_EPS))
        v = pad_rows(v)
        beta = _lane_bcast(bgp, h)
        gc = _lane_bcast(g_col, n_heads + h)
        gr = jnp.broadcast_to(g_row[n_heads + h:n_heads + h + 1, :], (C, C))
        decay = jnp.exp(jnp.where(causal, gc - gr, NEG_BIG))
        kb = k * beta
        kq = _bdot_nt(jnp.concatenate([kb, q], axis=0), k)
        low_l.append(kq[:C] * jnp.where(strict, decay, 0.0))
        qk_l.append(kq[C:] * decay)
        rhs_l.append(jnp.concatenate([v * beta, kb * _lane_bcast(eg_all, n_heads + h)], axis=1))
        q_l.append(q)
        k_l.append(k)

    def level_mask(lvl):
        return ((ri >> (lvl + 1)) == (ci >> (lvl + 1))) & ((ri >> lvl) != (ci >> lvl))

    m0 = level_mask(0)
    t_l = [eye - jnp.where(m0, low, 0.0) for low in low_l]
    for lvl in range(1, n_levels):
        mk = level_mask(lvl)
        et_l = [_bdot(jnp.where(mk, low, 0.0), t) for low, t in zip(low_l, t_l)]
        t_l = [t - _bdot(t, et) for t, et in zip(t_l, et_l)]

    sol_l = [_bdot(t, rhs) for t, rhs in zip(t_l, rhs_l)]
    z = z_ref[...]
    dn = dn_ref[...]
    for h in heads:
        val, kcd = sol_l[h][:, :dv], sol_l[h][:, dv:]
        st = st_ref[h]
        q_dec = q_l[h] * _lane_bcast(eg_all, n_heads + h)
        ks = _bdot(jnp.concatenate([kcd, q_dec], axis=0), st)
        v_new = val - ks[:C]
        o = ks[C:] + _bdot(qk_l[h], v_new)
        k_dec = k_l[h] * _lane_bcast(ekd_all, n_heads + h)
        st_ref[h] = st * egl_all[:, n_heads + h:n_heads + h + 1] + _bdot(k_dec.T, v_new)
        o = o[:cin, :]
        zh = z[:, h * dv:(h + 1) * dv]
        o = o * lax.rsqrt(jnp.mean(o * o, axis=-1, keepdims=True) + RMS_EPS) * dn * (zh * _sigmoid(zh))
        o_ref[:, h * dv:(h + 1) * dv] = o.astype(o_ref.dtype)

    @pl.when(c_id == pl.num_programs(1) - 1)
    def _():
        s_out_ref[...] = st_ref[...]


def _gdn(hmain, bg, buf8, s0, cw8, dn, layer, *, batch, rows_per_seq, n_valid, n_heads, dk, dv,
         conv_w, off_conv, off_z, out_dtype):
    m = hmain.shape[0]
    width = n_heads * dk
    cin = min(rows_per_seq, GDN_CHUNK)
    n_chunks = rows_per_seq // cin
    assert off_conv % (3 * width) == 0 and off_z % width == 0 and dk == dv == LANE == GDN_CHUNK
    return pl.pallas_call(
        functools.partial(_gdn_kernel, cin=cin, n_valid=n_valid, n_heads=n_heads, dk=dk, dv=dv,
                          conv_w=conv_w),
        grid=(batch, n_chunks),
        in_specs=[pl.BlockSpec((cin, 3 * width), lambda b, c: (b * n_chunks + c, off_conv // (3 * width))),
                  pl.BlockSpec((cin, width), lambda b, c: (b * n_chunks + c, off_z // width)),
                  pl.BlockSpec((cin, LANE), lambda b, c: (b * n_chunks + c, 0)),
                  pl.BlockSpec((None, SUBLANE, 3 * width), lambda b, c: (b, 0, 0)),
                  pl.BlockSpec((None, n_heads, dk, dv), lambda b, c: (b, 0, 0, 0)),
                  pl.BlockSpec((None, SUBLANE, 3 * width), lambda b, c: (layer, 0, 0)),
                  pl.BlockSpec((None, 1, dv), lambda b, c: (layer, 0, 0))],
        out_specs=[pl.BlockSpec((cin, width), lambda b, c: (b * n_chunks + c, 0)),
                   pl.BlockSpec((None, n_heads, dk, dv), lambda b, c: (b, 0, 0, 0))],
        out_shape=[jax.ShapeDtypeStruct((m, width), out_dtype),
                   jax.ShapeDtypeStruct((batch, n_heads, dk, dv), F32)],
        scratch_shapes=[pltpu.VMEM((SUBLANE, 3 * width), F32),
                        pltpu.VMEM((n_heads, dk, dv), F32)],
        compiler_params=_cparams(("parallel", "arbitrary"), 48),
        name="gdn",
    )(hmain, hmain, bg, buf8, s0, cw8, dn)


def _layernorm(y, g_ref, b_ref):
    mu = jnp.mean(y, axis=-1, keepdims=True)
    yc = y - mu
    var = jnp.mean(yc * yc, axis=-1, keepdims=True)
    return yc * lax.rsqrt(var + LN_EPS) * g_ref[...] + b_ref[...]


def _merge_proj_ln_kernel(oa_ref, od_ref, wa_ref, wb_ref, ga_ref, gb_ref, wo_ref, r_ref, g_ref, b_ref,
                          o_ref, *, alpha):
    j = pl.program_id(1)

    @pl.when(j == 0)
    def _():
        o_ref[...] = alpha * r_ref[...]

    mixed = ga_ref[...] * _bdot(oa_ref[...], wa_ref[...]) + gb_ref[...] * _bdot(od_ref[...], wb_ref[...])
    o_ref[...] += _bdot(mixed, wo_ref[...])

    @pl.when(j == pl.num_programs(1) - 1)
    def _():
        o_ref[...] = _layernorm(o_ref[...], g_ref, b_ref)


def _merge_proj_ln(oa, od, w_ba, w_bb, sg, w_out, resid, ln_g, ln_b, layer, *, alpha, tm, tn):
    m, wa = oa.shape
    wb = od.shape[1]
    d = w_out.shape[2]
    nj = d // tn
    vec = pl.BlockSpec((None, 1, d), lambda i, j: (layer, 0, 0))
    row = pl.BlockSpec((tm, d), lambda i, j: (i, 0))
    return pl.pallas_call(
        functools.partial(_merge_proj_ln_kernel, alpha=alpha),
        grid=(m // tm, nj),
        in_specs=[pl.BlockSpec((tm, wa), lambda i, j: (i, 0)),
                  pl.BlockSpec((tm, wb), lambda i, j: (i, 0)),
                  pl.BlockSpec((None, wa, tn), lambda i, j: (layer, 0, j)),
                  pl.BlockSpec((None, wb, tn), lambda i, j: (layer, 0, j)),
                  pl.BlockSpec((tm, tn), lambda i, j: (i, j)),
                  pl.BlockSpec((tm, tn), lambda i, j: (i, nj + j)),
                  pl.BlockSpec((None, tn, d), lambda i, j: (layer, j, 0)),
                  pl.BlockSpec((tm, d), lambda i, j: (i, 0), pipeline_mode=pl.Buffered(1)),
                  vec, vec],
        out_specs=row,
        out_shape=jax.ShapeDtypeStruct((m, d), F32),
        compiler_params=_cparams(("parallel", "arbitrary"), 56),
        name="merge_proj_ln",
    )(oa, od, w_ba, w_bb, sg, sg, w_out, resid, ln_g, ln_b)


def _mlp_ln_kernel(x_ref, wu_ref, wd_ref, g_ref, b_ref, o_ref, xb_ref, *, alpha):
    f = pl.program_id(1)

    @pl.when(f == 0)
    def _():
        x = x_ref[...]
        xb_ref[...] = x.astype(BF16)
        o_ref[...] = alpha * x

    hid = jnp.maximum(jnp.dot(xb_ref[...], wu_ref[...].astype(BF16), preferred_element_type=F32), 0.0)
    o_ref[...] += _bdot(hid * hid, wd_ref[...])

    @pl.when(f == pl.num_programs(1) - 1)
    def _():
        o_ref[...] = _layernorm(o_ref[...], g_ref, b_ref)


def _mlp_ln(x, w_up, w_down, ln_g, ln_b, layer, *, alpha, tm, tf):
    m, d = x.shape
    dff = w_up.shape[2]
    vec = pl.BlockSpec((None, 1, d), lambda i, f: (layer, 0, 0))
    row = pl.BlockSpec((tm, d), lambda i, f: (i, 0))
    return pl.pallas_call(
        functools.partial(_mlp_ln_kernel, alpha=alpha),
        grid=(m // tm, dff // tf),
        in_specs=[row,
                  pl.BlockSpec((None, d, tf), lambda i, f: (layer, 0, f)),
                  pl.BlockSpec((None, tf, d), lambda i, f: (layer, f, 0)),
                  vec, vec],
        out_specs=row,
        out_shape=jax.ShapeDtypeStruct((m, d), F32),
        scratch_shapes=[pltpu.VMEM((tm, d), BF16)],
        compiler_params=_cparams(("parallel", "arbitrary"), 58),
        name="mlp_ln",
    )(x, w_up, w_down, ln_g, ln_b)


def _rope_tables(pos, hd, rot):
    half = rot // 2
    inv_freq = ROPE_THETA ** (-jnp.arange(half, dtype=F32) * 2.0 / rot)
    ang = pos.astype(F32)[:, None] * inv_freq[None, :]
    cos, sin = jnp.cos(ang), jnp.sin(ang)
    lane = jnp.arange(LANE)
    within = lane % hd
    idx = within % half
    cos_l, sin_l = cos[:, idx], sin[:, idx]
    c = jnp.where(within < rot, cos_l, 1.0)
    sa = jnp.where(within < half, -sin_l, 0.0)
    sb = jnp.where((within >= half) & (within < rot), sin_l, 0.0)
    return c, sa, sb


def _lane_vec(v):
    return jnp.broadcast_to(v.astype(F32)[:, None, None], (v.shape[0], 1, LANE))


def kernel(x_prompt, x_sample, cache_k, cache_v, state_ssm, state_conv, page_table, w_in, conv_w, a_log, dt_bias, delta_norm_w, lambda_q1, lambda_k1, lambda_q2, lambda_k2, attn_norm_w, w_branch_a, w_branch_b, w_out, ln1_g, ln1_b, w_mlp_up, w_mlp_down, ln2_g, ln2_b):
    depth, d_model, in_cols = w_in.shape
    bp, sp, _ = x_prompt.shape
    bs, ss, _ = x_sample.shape
    _, n_pool, page, ha, _, hd = cache_k.shape
    _, _, hdn, dk, dv = state_ssm.shape
    dvh = cache_v.shape[-1]
    cw_len = conv_w.shape[1]
    n_pages = page_table.shape[1]
    past_len = n_pages * page
    qk_w = ha * 2 * hd
    wa = ha * dvh
    conv_ch = conv_w.shape[2]
    wdv = hdn * dv
    rot = hd // 4
    off_k, off_v, off_conv = qk_w, 2 * qk_w, 2 * qk_w + wa
    off_z = off_conv + conv_ch
    n_main = off_z + wdv
    off_gate = n_main + 2 * hdn
    assert in_cols == off_gate + 2 * d_model and 2 * hd == LANE and dvh == LANE
    assert qk_w == wa == wdv and conv_ch == 3 * wdv and cw_len <= SUBLANE
    alpha = (2 * depth) ** 0.25
    rows_s = SUBLANE

    w_t = jnp.swapaxes(w_in, 1, 2)
    cache_kt = jnp.transpose(cache_k, (0, 1, 3, 4, 5, 2)).reshape(depth, n_pool, qk_w, page)
    cache_v2 = cache_v.reshape(depth, n_pool, page * ha, dvh)

    alog = jnp.pad(a_log.astype(F32), ((0, 0), (hdn, LANE - 2 * hdn)))[:, None, :]
    dtb = jnp.pad(dt_bias.astype(F32), ((0, 0), (hdn, LANE - 2 * hdn)))[:, None, :]
    cw8 = jnp.pad(conv_w.astype(F32), ((0, 0), (0, SUBLANE - cw_len), (0, 0)))
    dn = delta_norm_w.astype(F32)[:, None, :]
    nw = attn_norm_w.astype(F32)[:, None, :]
    vec3 = lambda a: a.astype(F32)[:, None, :]
    g1, b1, g2, b2 = vec3(ln1_g), vec3(ln1_b), vec3(ln2_g), vec3(ln2_b)
    lam_init = jnp.asarray([0.8 - 0.6 * math.exp(-0.3 * l) for l in range(depth)], F32)
    lam = (jnp.exp(jnp.sum(lambda_q1.astype(F32) * lambda_k1.astype(F32), axis=-1))
           - jnp.exp(jnp.sum(lambda_q2.astype(F32) * lambda_k2.astype(F32), axis=-1)) + lam_init)
    lamv = _lane_vec(lam)

    tabs_p = _rope_tables(jnp.tile(jnp.arange(sp), bp), hd, rot)
    tabs_s = _rope_tables(jnp.tile(past_len + jnp.arange(rows_s), bs), hd, rot)

    mp = bp * sp
    ms = bs * rows_s
    xs_pad = jnp.pad(x_sample, ((0, 0), (0, rows_s - ss), (0, 0))).reshape(ms, d_model)
    xp = x_prompt.reshape(mp, d_model)
    state = {"p": xp, "s": xs_pad}
    buf_p = jnp.zeros((bp, SUBLANE, conv_ch), F32)
    s0_p = jnp.zeros((bp, hdn, dk, dv), F32)

    tn_main = _pick(math.gcd(qk_w, n_main), (1024, 512, 256, 128))
    tn_d = _pick(math.gcd(2 * d_model, n_main), (1024, 512, 256, 128))
    tn_mix = _pick(d_model, (256, 128))
    tf = _pick(w_mlp_up.shape[2], (512, 256, 128))
    tq = _pick(sp, (1024, 512, 256, 128))
    tk_attn = min(tq, 512)
    pp = _pick(n_pages, (16, 8, 4, 2, 1))
    tm_of = {"p": _pick(mp, (1024, 512, 256, 128)), "s": ms}

    outs = {"p": ([], [], [], []), "s": ([], [], [], [])}
    for l in range(depth):
        post = 1.0 - (0.8 - 0.6 * math.exp(-0.3 * l))
        for grp in ("p", "s"):
            xb = state[grp]
            tm = tm_of[grp]
            tabs = tabs_p if grp == "p" else tabs_s
            hmain = _inproj(xb, w_t, l, tabs, n_main=n_main, n_rope=off_v, half=rot // 2,
                            tm=tm, tn=tn_main)
            sg, bg = _gates(xb, w_t, l, alog, dtb, off_small=n_main, n_heads=hdn, n_out=2 * d_model,
                            tm=tm, tn=tn_d)
            if grp == "p":
                oa = _attn_prompt(hmain, lamv, nw, l, batch=bp, seq=sp, n_heads=ha, hd=hd,
                                  post_scale=post, tq=tq, tk=tk_attn, out_dtype=BF16)
                od, ssm = _gdn(hmain, bg, buf_p, s0_p, cw8, dn, l, batch=bp, rows_per_seq=sp,
                               n_valid=sp, n_heads=hdn, dk=dk, dv=dv, conv_w=cw_len,
                               off_conv=off_conv, off_z=off_z, out_dtype=BF16)
                h3 = hmain.reshape(bp, sp, n_main)
                k_new = h3[:, :, off_k:off_v]
                v_new = h3[:, :, off_v:off_conv]
                conv_new = h3[:, sp - (cw_len - 1):, off_conv:off_z]
            else:
                oa = _attn_decode(hmain, cache_kt, cache_v2, page_table, lamv, nw, l, n_heads=ha,
                                  hd=hd, n_tok=ss, post_scale=post, pp=pp)
                buf_s = jnp.pad(state_conv[l].astype(F32),
                                ((0, 0), (SUBLANE - (cw_len - 1), 0), (0, 0)))
                od, ssm = _gdn(hmain, bg, buf_s, state_ssm[l].astype(F32), cw8, dn, l, batch=bs,
                               rows_per_seq=rows_s, n_valid=ss, n_heads=hdn, dk=dk, dv=dv,
                               conv_w=cw_len, off_conv=off_conv, off_z=off_z, out_dtype=F32)
                h3 = hmain.reshape(bs, rows_s, n_main)
                k_new = h3[:, :ss, off_k:off_v]
                v_new = h3[:, :ss, off_v:off_conv]
                conv_new = h3[:, ss - (cw_len - 1):ss, off_conv:off_z]
            x1 = _merge_proj_ln(oa, od, w_branch_a, w_branch_b, sg, w_out, xb, g1, b1, l, alpha=alpha,
                                tm=tm, tn=tn_mix)
            state[grp] = _mlp_ln(x1, w_mlp_up, w_mlp_down, g2, b2, l, alpha=alpha, tm=tm, tf=tf)
            ko, vo, so, co = outs[grp]
            nb = k_new.shape[0]
            ko.append(k_new.reshape(nb, -1, ha, 2, hd))
            vo.append(v_new.reshape(nb, -1, ha, dvh))
            so.append(ssm)
            co.append(conv_new)

    y_p = state["p"].reshape(bp, sp, d_model)
    y_s = state["s"].reshape(bs, rows_s, d_model)[:, :ss]
    kp, vp, sp_l, cp = (jnp.stack(t) for t in outs["p"])
    ks, vs, ss_l, cs = (jnp.stack(t) for t in outs["s"])
    return (y_p, y_s, kp, vp, sp_l, cp, ks, vs, ss_l, cs)
```

```python
import functools
import math

import jax
import jax.numpy as jnp
from jax import lax
from jax.experimental import pallas as pl
from jax.experimental.pallas import tpu as pltpu

F32 = jnp.float32
BF16 = jnp.bfloat16
LANE = 128
SUBLANE = 8
MIB = 1024 * 1024

ROPE_THETA = 500000.0
LN_EPS = 1e-5
RMS_EPS = 1e-6
L2_EPS = 1e-6
GDN_CHUNK = 128
GDN_CHUNKS_PER_STEP = 2
NEG_BIG = -1e30


def _cparams(sem, vmem_mib):
    return pltpu.CompilerParams(dimension_semantics=sem, vmem_limit_bytes=vmem_mib * MIB)


def _pick(n, cands):
    for c in cands:
        if n % c == 0:
            return c
    raise ValueError(f"no tile for {n} in {cands}")


def _log2(n):
    assert n & (n - 1) == 0
    return n.bit_length() - 1


def _sigmoid(x):
    return 1.0 / (1.0 + jnp.exp(-x))


def _bdot(a, b):
    return jnp.dot(a.astype(BF16), b.astype(BF16), preferred_element_type=F32)


def _bdot_nt(a, b):
    return lax.dot_general(a.astype(BF16), b.astype(BF16), (((1,), (1,)), ((), ())),
                           preferred_element_type=F32)


def _split3(a):
    a1 = a.astype(BF16)
    r1 = a - a1.astype(F32)
    a2 = r1.astype(BF16)
    a3 = (r1 - a2.astype(F32)).astype(BF16)
    return a1, a2, a3


def _dot_exact_lhs(mask_bf16, b):
    d = functools.partial(jnp.dot, preferred_element_type=F32)
    b1, b2, b3 = _split3(b)
    return d(mask_bf16, b1) + (d(mask_bf16, b2) + d(mask_bf16, b3))


def _lane_bcast(x, j):
    return jnp.broadcast_to(x[:, j:j + 1], x.shape)


def _inproj_kernel(x_ref, w_ref, c_ref, sa_ref, sb_ref, o_ref, xb_ref, *, n_rope_blocks, tn, half):
    j = pl.program_id(1)

    @pl.when(j == 0)
    def _():
        xb_ref[...] = x_ref[...].astype(BF16)

    acc = _bdot_nt(xb_ref[...], w_ref[...])

    @pl.when(j >= n_rope_blocks)
    def _():
        o_ref[...] = acc

    @pl.when(j < n_rope_blocks)
    def _():
        c, sa, sb = c_ref[...], sa_ref[...], sb_ref[...]
        for g in range(tn // LANE):
            xg = acc[:, g * LANE:(g + 1) * LANE]
            o_ref[:, g * LANE:(g + 1) * LANE] = (
                xg * c + pltpu.roll(xg, LANE - half, 1) * sa + pltpu.roll(xg, half, 1) * sb)


def _inproj(x, w_t, layer, tabs, *, n_main, n_rope, half, tm, tn):
    m, d = x.shape
    c, sa, sb = tabs
    tab_spec = pl.BlockSpec((tm, LANE), lambda i, j: (i, 0))
    return pl.pallas_call(
        functools.partial(_inproj_kernel, n_rope_blocks=n_rope // tn, tn=tn, half=half),
        grid=(m // tm, n_main // tn),
        in_specs=[pl.BlockSpec((tm, d), lambda i, j: (i, 0)),
                  pl.BlockSpec((None, tn, d), lambda i, j: (layer, j, 0)),
                  tab_spec, tab_spec, tab_spec],
        out_specs=pl.BlockSpec((tm, tn), lambda i, j: (i, j)),
        out_shape=jax.ShapeDtypeStruct((m, n_main), F32),
        scratch_shapes=[pltpu.VMEM((tm, d), BF16)],
        compiler_params=_cparams(("parallel", "arbitrary"), 56),
        name="inproj",
    )(x, w_t, c, sa, sb)


def _gates_kernel(x_ref, wa_ref, wb_ref, alog_ref, dtb_ref, o_ref, bg_ref, xb_ref, *, skip, n_heads):
    j = pl.program_id(1)

    @pl.when(j == 0)
    def _():
        xb_ref[...] = x_ref[...].astype(BF16)
        w = wa_ref[0:skip, :]
        w = jnp.concatenate([w, jnp.zeros((LANE - skip, w.shape[1]), w.dtype)], axis=0)
        h = _bdot_nt(xb_ref[...], w)
        lane = lax.broadcasted_iota(jnp.int32, h.shape, 1)
        beta = _sigmoid(h)
        t = h + dtb_ref[...]
        softplus = jnp.maximum(t, 0.0) + jnp.log1p(jnp.exp(-jnp.abs(t)))
        g = -jnp.exp(alog_ref[...]) * softplus
        bg_ref[...] = jnp.where(lane < n_heads, beta, jnp.where(lane < 2 * n_heads, g, 0.0))

    w = jnp.concatenate([wa_ref[skip:, :].astype(BF16), wb_ref[...].astype(BF16)], axis=0)
    o_ref[...] = _sigmoid(_bdot_nt(xb_ref[...], w))


def _gates(x, w_t, layer, alog, dtb, *, off_small, n_heads, n_out, tm, tn):
    m, d = x.shape
    skip = 2 * n_heads
    assert off_small % tn == 0 and tn % skip == 0 and skip % SUBLANE == 0 and skip <= LANE
    vec = pl.BlockSpec((None, 1, LANE), lambda i, j: (layer, 0, 0))
    return pl.pallas_call(
        functools.partial(_gates_kernel, skip=skip, n_heads=n_heads),
        grid=(m // tm, n_out // tn),
        in_specs=[pl.BlockSpec((tm, d), lambda i, j: (i, 0)),
                  pl.BlockSpec((None, tn, d), lambda i, j: (layer, off_small // tn + j, 0)),
                  pl.BlockSpec((None, skip, d),
                               lambda i, j: (layer, (off_small + (j + 1) * tn) // skip, 0)),
                  vec, vec],
        out_specs=[pl.BlockSpec((tm, tn), lambda i, j: (i, j)),
                   pl.BlockSpec((tm, LANE), lambda i, j: (i, 0))],
        out_shape=[jax.ShapeDtypeStruct((m, n_out), F32), jax.ShapeDtypeStruct((m, LANE), F32)],
        scratch_shapes=[pltpu.VMEM((tm, d), BF16)],
        compiler_params=_cparams(("parallel", "arbitrary"), 56),
        name="gates",
    )(x, w_t, w_t, alog, dtb)


def _rms_head(o, nw, post_scale):
    return o * lax.rsqrt(jnp.mean(o * o, axis=-1, keepdims=True) + RMS_EPS) * nw * post_scale


def _attn_prompt_kernel(q_ref, k_ref, v_ref, lam_ref, nw_ref, o_ref, qm_ref,
                        *, tq, tk, sub, hd, scale, post_scale):
    qi = pl.program_id(2)
    q = q_ref[...] * scale
    lane = lax.broadcasted_iota(jnp.int32, q.shape, 1)
    qm_ref[0] = jnp.where(lane < hd, q, 0.0).astype(BF16)
    qm_ref[1] = jnp.where(lane >= hd, q, 0.0).astype(BF16)
    row = lax.broadcasted_iota(jnp.int32, (sub, tk), 0)
    col = lax.broadcasted_iota(jnp.int32, (sub, tk), 1)
    nr = tq // sub
    parts = [(mp, r) for mp in range(2) for r in range(nr)]
    ones = jnp.ones((tk, LANE), BF16)
    n_full = (qi * tq) // tk

    def block(kb, d, carry):
        ms, accs = carry
        start = pl.multiple_of(kb * tk, tk)
        k = k_ref[pl.ds(start, tk), :].astype(BF16)
        v1 = jnp.concatenate([v_ref[pl.ds(start, tk), :].astype(BF16), ones], axis=1)
        if d is None:
            live = list(range(len(parts)))
            masked = set()
        else:
            live = [i for i, (_, r) in enumerate(parts) if (r + 1) * sub - 1 >= d * tk]
            masked = {i for i in live if parts[i][1] * sub < (d + 1) * tk - 1}
        s_l = {i: _bdot_nt(qm_ref[parts[i][0], pl.ds(parts[i][1] * sub, sub), :], k) for i in live}
        for i in masked:
            s_l[i] = jnp.where(col + d * tk <= row + parts[i][1] * sub, s_l[i], NEG_BIG)
        mn_l = {i: jnp.maximum(ms[i], jnp.max(s_l[i], axis=-1, keepdims=True)) for i in live}
        p_l = {i: jnp.exp(s_l[i] - mn_l[i]).astype(BF16) for i in live}
        al_l = {i: jnp.exp(ms[i] - mn_l[i]) for i in live}
        pv_l = {i: jnp.dot(p_l[i], v1, preferred_element_type=F32) for i in live}
        ms = tuple(mn_l.get(i, ms[i]) for i in range(len(parts)))
        accs = tuple(al_l[i] * accs[i] + pv_l[i] if i in pv_l else accs[i] for i in range(len(parts)))
        return ms, accs

    carry = (tuple(jnp.full((sub, 1), NEG_BIG, F32) for _ in parts),
             tuple(jnp.zeros((sub, 2 * LANE), F32) for _ in parts))
    carry = lax.fori_loop(0, n_full, lambda kb, c: block(kb, None, c), carry)
    for d in range(tq // tk):
        carry = block(n_full + d, d, carry)
    accs = carry[1]
    num = [jnp.concatenate([accs[mp * nr + r][:, :LANE] for r in range(nr)], axis=0) for mp in range(2)]
    den = [jnp.concatenate([accs[mp * nr + r][:, LANE:] for r in range(nr)], axis=0) for mp in range(2)]
    o = num[0] / den[0] - lam_ref[...] * (num[1] / den[1])
    o_ref[...] = _rms_head(o, nw_ref[...], post_scale).astype(o_ref.dtype)


def _attn_prompt(hmain, lamv, nw, layer, *, batch, seq, n_heads, hd, post_scale, tq, tk, out_dtype):
    m = hmain.shape[0]
    nq = seq // tq
    sub = min(tk, LANE)
    assert tq % tk == 0 and tk % sub == 0
    vec = pl.BlockSpec((None, 1, LANE), lambda b, h, i: (layer, 0, 0))
    return pl.pallas_call(
        functools.partial(_attn_prompt_kernel, tq=tq, tk=tk, sub=sub, hd=hd, scale=hd ** -0.5,
                          post_scale=post_scale),
        grid=(batch, n_heads, nq),
        in_specs=[pl.BlockSpec((tq, LANE), lambda b, h, i: (b * nq + i, h)),
                  pl.BlockSpec((seq, LANE), lambda b, h, i: (b, n_heads + h)),
                  pl.BlockSpec((seq, LANE), lambda b, h, i: (b, 2 * n_heads + h)),
                  vec, vec],
        out_specs=pl.BlockSpec((tq, LANE), lambda b, h, i: (b * nq + i, h)),
        out_shape=jax.ShapeDtypeStruct((m, n_heads * LANE), out_dtype),
        scratch_shapes=[pltpu.VMEM((2, tq, LANE), BF16)],
        compiler_params=_cparams(("parallel", "parallel", "arbitrary"), 32),
        name="attn_prompt",
    )(hmain, hmain, hmain, lamv, nw)


def _attn_decode_kernel(pt_ref, q_ref, kn_ref, vn_ref, lam_ref, nw_ref, *rest,
                        pp, n_heads, hd, n_tok, page, scale, post_scale):
    k_refs, v_refs = rest[:pp], rest[pp:2 * pp]
    o_ref = rest[2 * pp]
    qbd_ref, m_ref, l_ref, acc_ref, kpad_ref, vpad_ref = rest[2 * pp + 1:]
    p_id = pl.program_id(1)
    rows = n_heads * SUBLANE
    width = n_heads * LANE

    def update(s_parts, v_parts):
        s = jnp.concatenate(s_parts, axis=1) if len(s_parts) > 1 else s_parts[0]
        m_old = m_ref[:, :1]
        mn = jnp.maximum(m_old, jnp.max(s, axis=-1, keepdims=True))
        p = jnp.exp(s - mn)
        al = jnp.exp(m_old - mn)
        l_ref[...] = jnp.broadcast_to(al * l_ref[:, :1] + jnp.sum(p, axis=-1, keepdims=True), (rows, LANE))
        m_ref[...] = jnp.broadcast_to(mn, (rows, LANE))
        pv = None
        for i, v in enumerate(v_parts):
            t = jnp.dot(p[:, i * page:(i + 1) * page].astype(BF16), v, preferred_element_type=F32)
            pv = t if pv is None else pv + t
        acc_ref[...] = al * acc_ref[...] + pv

    @pl.when(p_id == 0)
    def _():
        q = q_ref[...] * scale
        r8 = lax.broadcasted_iota(jnp.int32, q.shape, 0)
        q8 = jnp.where(r8 < n_tok, q, pltpu.roll(q, n_tok, 0))
        qt = jnp.concatenate([q8] * n_heads, axis=0)
        r = lax.broadcasted_iota(jnp.int32, qt.shape, 0)
        c = lax.broadcasted_iota(jnp.int32, qt.shape, 1)
        sel = (c >> _log2(hd)) == (r >> _log2(SUBLANE)) * 2 + ((r & (SUBLANE - 1)) >> _log2(n_tok))
        qbd_ref[...] = jnp.where(sel, qt, 0.0).astype(BF16)
        m_ref[...] = jnp.full(m_ref.shape, NEG_BIG, F32)
        l_ref[...] = jnp.zeros(l_ref.shape, F32)
        acc_ref[...] = jnp.zeros(acc_ref.shape, F32)
        kpad_ref[...] = jnp.zeros(kpad_ref.shape, F32)
        vpad_ref[...] = jnp.zeros(vpad_ref.shape, F32)
        kpad_ref[0:SUBLANE, :] = kn_ref[...]
        vpad_ref[0:SUBLANE, :] = vn_ref[...]
        s = _bdot_nt(qbd_ref[...], kpad_ref[...])
        rr = lax.broadcasted_iota(jnp.int32, s.shape, 0)
        cc = lax.broadcasted_iota(jnp.int32, s.shape, 1)
        s = jnp.where(cc <= (rr & (n_tok - 1)), s, NEG_BIG)
        update([s], [vpad_ref[...].astype(BF16)])

    qbd = qbd_ref[...]
    s_parts = [_bdot(qbd, k_refs[i][...]) for i in range(pp)]
    v_parts = [jnp.concatenate([v_refs[i][pl.ds(h, page, stride=n_heads), :] for h in range(n_heads)],
                               axis=1).astype(BF16) for i in range(pp)]
    update(s_parts, v_parts)

    @pl.when(p_id == pl.num_programs(1) - 1)
    def _():
        accn = acc_ref[...] / l_ref[:, :1]
        c = lax.broadcasted_iota(jnp.int32, (SUBLANE, width), 1)
        res = jnp.zeros((SUBLANE, width), F32)
        for h in range(n_heads):
            res = res + jnp.where((c >> _log2(LANE)) == h, accn[h * SUBLANE:(h + 1) * SUBLANE, :], 0.0)
        o8 = res - lam_ref[...][:, :1] * pltpu.roll(res, SUBLANE - n_tok, 0)
        nw = nw_ref[...]
        for h in range(n_heads):
            o_ref[:, h * LANE:(h + 1) * LANE] = _rms_head(o8[:, h * LANE:(h + 1) * LANE], nw, post_scale)


def _attn_decode(hmain, cache_kt, cache_v2, page_table, lamv, nw, layer, *, n_heads, hd, n_tok,
                 post_scale, pp):
    m = hmain.shape[0]
    batch, n_pages = page_table.shape
    page = cache_kt.shape[3]
    width = n_heads * LANE
    rows = n_heads * SUBLANE
    assert 2 * n_tok == SUBLANE and page == LANE and n_pages % pp == 0
    assert cache_kt.shape[2] == width and cache_v2.shape[2:] == (page * n_heads, LANE)
    vec = pl.BlockSpec((None, 1, LANE), lambda b, p, pt: (layer, 0, 0))

    def page_spec(i, shape):
        return pl.BlockSpec((None, None) + shape, lambda b, p, pt: (layer, pt[b, p * pp + i], 0, 0))

    in_specs = ([pl.BlockSpec((SUBLANE, width), lambda b, p, pt: (b, 0)),
                 pl.BlockSpec((SUBLANE, width), lambda b, p, pt: (b, 1)),
                 pl.BlockSpec((SUBLANE, width), lambda b, p, pt: (b, 2)),
                 vec, vec]
                + [page_spec(i, (width, page)) for i in range(pp)]
                + [page_spec(i, (page * n_heads, LANE)) for i in range(pp)])
    grid_spec = pltpu.PrefetchScalarGridSpec(
        num_scalar_prefetch=1,
        grid=(batch, n_pages // pp),
        in_specs=in_specs,
        out_specs=pl.BlockSpec((SUBLANE, width), lambda b, p, pt: (b, 0)),
        scratch_shapes=[pltpu.VMEM((rows, width), BF16),
                        pltpu.VMEM((rows, LANE), F32),
                        pltpu.VMEM((rows, LANE), F32),
                        pltpu.VMEM((rows, width), F32),
                        pltpu.VMEM((page, width), F32),
                        pltpu.VMEM((page, width), F32)])
    return pl.pallas_call(
        functools.partial(_attn_decode_kernel, pp=pp, n_heads=n_heads, hd=hd, n_tok=n_tok, page=page,
                          scale=hd ** -0.5, post_scale=post_scale),
        grid_spec=grid_spec,
        out_shape=jax.ShapeDtypeStruct((m, width), F32),
        compiler_params=_cparams(("parallel", "arbitrary"), 56),
        name="attn_decode",
    )(page_table, hmain, hmain, hmain, lamv, nw, *([cache_kt] * pp), *([cache_v2] * pp))


def _gdn_kernel(u_ref, z_ref, bg_ref, buf_ref, s0_ref, cw_ref, dn_ref, o_ref, s_out_ref,
                halo_ref, st_ref, *, cin, n_valid, n_heads, dk, dv, conv_w):
    c_id = pl.program_id(1)
    C = GDN_CHUNK
    width = n_heads * dk
    heads = range(n_heads)

    @pl.when(c_id == 0)
    def _():
        halo_ref[...] = buf_ref[...]
        st_ref[...] = s0_ref[...]

    u = u_ref[...]
    ext = jnp.concatenate([halo_ref[...], u], axis=0)
    cw = cw_ref[...]
    conv = None
    for i in range(conv_w):
        sh = SUBLANE - (conv_w - 1) + i
        term = ext[sh:sh + cin, :] * cw[i:i + 1, :]
        conv = term if conv is None else conv + term
    halo_ref[...] = u[cin - SUBLANE:cin, :]
    act = conv * _sigmoid(conv)
    bg = bg_ref[...]
    if n_valid < cin:
        valid = lax.broadcasted_iota(jnp.int32, (cin, 1), 0) < n_valid
        act = jnp.where(valid, act, 0.0)
        bg = jnp.where(valid, bg, 0.0)

    rows = min(cin, C)
    n_sub = cin // rows

    def pad_rows(a):
        if rows == C:
            return a
        return jnp.concatenate([a, jnp.zeros((C - rows, a.shape[1]), a.dtype)], axis=0)

    ri = lax.broadcasted_iota(jnp.int32, (C, C), 0)
    ci = lax.broadcasted_iota(jnp.int32, (C, C), 1)
    causal = ri >= ci
    strict = ri > ci
    tril = jnp.where(causal, 1.0, 0.0).astype(BF16)
    eye = jnp.where(ri == ci, 1.0, 0.0).astype(F32)
    n_levels = _log2(C)

    def level_mask(lvl):
        return ((ri >> (lvl + 1)) == (ci >> (lvl + 1))) & ((ri >> lvl) != (ci >> lvl))

    eg_s, ekd_s, egl_s = [], [], []
    q_l, k_l, low_l, qk_l, rhs_l = [], [], [], [], []
    for s in range(n_sub):
        rs = slice(s * rows, (s + 1) * rows)
        bgp = pad_rows(bg[rs])
        g_col = _dot_exact_lhs(tril, bgp)
        g_row = g_col.T
        g_last = g_col[C - 1:C, :]
        eg_all = jnp.exp(g_col)
        eg_s.append(eg_all)
        ekd_s.append(jnp.exp(g_last - g_col))
        egl_s.append(jnp.exp(g_last))
        for h in heads:
            q = act[rs, h * dk:(h + 1) * dk]
            k = act[rs, width + h * dk:width + (h + 1) * dk]
            v = act[rs, 2 * width + h * dv:2 * width + (h + 1) * dv]
            q = pad_rows(q * lax.rsqrt(jnp.sum(q * q, axis=-1, keepdims=True) + L2_EPS) * (dk ** -0.5))
            k = pad_rows(k * lax.rsqrt(jnp.sum(k * k, axis=-1, keepdims=True) + L2_EPS))
            v = pad_rows(v)
            beta = _lane_bcast(bgp, h)
            gc = _lane_bcast(g_col, n_heads + h)
            gr = jnp.broadcast_to(g_row[n_heads + h:n_heads + h + 1, :], (C, C))
            decay = jnp.exp(jnp.where(causal, gc - gr, NEG_BIG))
            kb = k * beta
            kq = _bdot_nt(jnp.concatenate([kb, q], axis=0), k)
            low_l.append(kq[:C] * jnp.where(strict, decay, 0.0))
            qk_l.append(kq[C:] * decay)
            rhs_l.append(jnp.concatenate([v * beta, kb * _lane_bcast(eg_all, n_heads + h)], axis=1))
            q_l.append(q)
            k_l.append(k)

    m0 = level_mask(0)
    t_l = [eye - jnp.where(m0, low, 0.0) for low in low_l]
    for lvl in range(1, n_levels):
        mk = level_mask(lvl)
        et_l = [_bdot(jnp.where(mk, low, 0.0), t) for low, t in zip(low_l, t_l)]
        t_l = [t - _bdot(t, et) for t, et in zip(t_l, et_l)]
    sol_l = [_bdot(t, rhs) for t, rhs in zip(t_l, rhs_l)]

    z = z_ref[...]
    dn = dn_ref[...]
    for s in range(n_sub):
        for h in heads:
            i = s * n_heads + h
            val, kcd = sol_l[i][:, :dv], sol_l[i][:, dv:]
            st = st_ref[h]
            q_dec = q_l[i] * _lane_bcast(eg_s[s], n_heads + h)
            ks = _bdot(jnp.concatenate([kcd, q_dec], axis=0), st)
            v_new = val - ks[:C]
            o = ks[C:] + _bdot(qk_l[i], v_new)
            k_dec = k_l[i] * _lane_bcast(ekd_s[s], n_heads + h)
            st_ref[h] = st * egl_s[s][:, n_heads + h:n_heads + h + 1] + _bdot(k_dec.T, v_new)
            o = o[:rows, :]
            zh = z[s * rows:(s + 1) * rows, h * dv:(h + 1) * dv]
            o = o * lax.rsqrt(jnp.mean(o * o, axis=-1, keepdims=True) + RMS_EPS) * dn * (zh * _sigmoid(zh))
            o_ref[s * rows:(s + 1) * rows, h * dv:(h + 1) * dv] = o.astype(o_ref.dtype)

    @pl.when(c_id == pl.num_programs(1) - 1)
    def _():
        s_out_ref[...] = st_ref[...]


def _gdn(hmain, bg, buf8, s0, cw8, dn, layer, *, batch, rows_per_seq, n_valid, n_heads, dk, dv,
         conv_w, off_conv, off_z, out_dtype):
    m = hmain.shape[0]
    width = n_heads * dk
    cin = min(rows_per_seq, GDN_CHUNK * GDN_CHUNKS_PER_STEP)
    assert cin <= GDN_CHUNK or cin % GDN_CHUNK == 0
    n_chunks = rows_per_seq // cin
    assert off_conv % (3 * width) == 0 and off_z % width == 0 and dk == dv == LANE == GDN_CHUNK
    return pl.pallas_call(
        functools.partial(_gdn_kernel, cin=cin, n_valid=n_valid, n_heads=n_heads, dk=dk, dv=dv,
                          conv_w=conv_w),
        grid=(batch, n_chunks),
        in_specs=[pl.BlockSpec((cin, 3 * width), lambda b, c: (b * n_chunks + c, off_conv // (3 * width))),
                  pl.BlockSpec((cin, width), lambda b, c: (b * n_chunks + c, off_z // width)),
                  pl.BlockSpec((cin, LANE), lambda b, c: (b * n_chunks + c, 0)),
                  pl.BlockSpec((None, SUBLANE, 3 * width), lambda b, c: (b, 0, 0)),
                  pl.BlockSpec((None, n_heads, dk, dv), lambda b, c: (b, 0, 0, 0)),
                  pl.BlockSpec((None, SUBLANE, 3 * width), lambda b, c: (layer, 0, 0)),
                  pl.BlockSpec((None, 1, dv), lambda b, c: (layer, 0, 0))],
        out_specs=[pl.BlockSpec((cin, width), lambda b, c: (b * n_chunks + c, 0)),
                   pl.BlockSpec((None, n_heads, dk, dv), lambda b, c: (b, 0, 0, 0))],
        out_shape=[jax.ShapeDtypeStruct((m, width), out_dtype),
                   jax.ShapeDtypeStruct((batch, n_heads, dk, dv), F32)],
        scratch_shapes=[pltpu.VMEM((SUBLANE, 3 * width), F32),
                        pltpu.VMEM((n_heads, dk, dv), F32)],
        compiler_params=_cparams(("parallel", "arbitrary"), 48),
        name="gdn",
    )(hmain, hmain, bg, buf8, s0, cw8, dn)


def _layernorm(y, g_ref, b_ref):
    mu = jnp.mean(y, axis=-1, keepdims=True)
    yc = y - mu
    var = jnp.mean(yc * yc, axis=-1, keepdims=True)
    return yc * lax.rsqrt(var + LN_EPS) * g_ref[...] + b_ref[...]


def _merge_proj_ln_kernel(oa_ref, od_ref, wa_ref, wb_ref, ga_ref, gb_ref, wo_ref, r_ref, g_ref, b_ref,
                          o_ref, *, alpha):
    j = pl.program_id(1)

    @pl.when(j == 0)
    def _():
        o_ref[...] = alpha * r_ref[...]

    mixed = ga_ref[...] * _bdot(oa_ref[...], wa_ref[...]) + gb_ref[...] * _bdot(od_ref[...], wb_ref[...])
    o_ref[...] += _bdot(mixed, wo_ref[...])

    @pl.when(j == pl.num_programs(1) - 1)
    def _():
        o_ref[...] = _layernorm(o_ref[...], g_ref, b_ref)


def _merge_proj_ln(oa, od, w_ba, w_bb, sg, w_out, resid, ln_g, ln_b, layer, *, alpha, tm, tn):
    m, wa = oa.shape
    wb = od.shape[1]
    d = w_out.shape[2]
    nj = d // tn
    vec = pl.BlockSpec((None, 1, d), lambda i, j: (layer, 0, 0))
    row = pl.BlockSpec((tm, d), lambda i, j: (i, 0))
    return pl.pallas_call(
        functools.partial(_merge_proj_ln_kernel, alpha=alpha),
        grid=(m // tm, nj),
        in_specs=[pl.BlockSpec((tm, wa), lambda i, j: (i, 0)),
                  pl.BlockSpec((tm, wb), lambda i, j: (i, 0)),
                  pl.BlockSpec((None, wa, tn), lambda i, j: (layer, 0, j)),
                  pl.BlockSpec((None, wb, tn), lambda i, j: (layer, 0, j)),
                  pl.BlockSpec((tm, tn), lambda i, j: (i, j)),
                  pl.BlockSpec((tm, tn), lambda i, j: (i, nj + j)),
                  pl.BlockSpec((None, tn, d), lambda i, j: (layer, j, 0)),
                  pl.BlockSpec((tm, d), lambda i, j: (i, 0), pipeline_mode=pl.Buffered(1)),
                  vec, vec],
        out_specs=row,
        out_shape=jax.ShapeDtypeStruct((m, d), F32),
        compiler_params=_cparams(("parallel", "arbitrary"), 56),
        name="merge_proj_ln",
    )(oa, od, w_ba, w_bb, sg, sg, w_out, resid, ln_g, ln_b)


def _mlp_ln_kernel(x_ref, wu_ref, wd_ref, g_ref, b_ref, o_ref, xb_ref, *, alpha):
    f = pl.program_id(1)

    @pl.when(f == 0)
    def _():
        x = x_ref[...]
        xb_ref[...] = x.astype(BF16)
        o_ref[...] = alpha * x

    hid = jnp.maximum(jnp.dot(xb_ref[...], wu_ref[...].astype(BF16), preferred_element_type=F32), 0.0)
    o_ref[...] += _bdot(hid * hid, wd_ref[...])

    @pl.when(f == pl.num_programs(1) - 1)
    def _():
        o_ref[...] = _layernorm(o_ref[...], g_ref, b_ref)


def _mlp_ln(x, w_up, w_down, ln_g, ln_b, layer, *, alpha, tm, tf):
    m, d = x.shape
    dff = w_up.shape[2]
    vec = pl.BlockSpec((None, 1, d), lambda i, f: (layer, 0, 0))
    row = pl.BlockSpec((tm, d), lambda i, f: (i, 0))
    return pl.pallas_call(
        functools.partial(_mlp_ln_kernel, alpha=alpha),
        grid=(m // tm, dff // tf),
        in_specs=[row,
                  pl.BlockSpec((None, d, tf), lambda i, f: (layer, 0, f)),
                  pl.BlockSpec((None, tf, d), lambda i, f: (layer, f, 0)),
                  vec, vec],
        out_specs=row,
        out_shape=jax.ShapeDtypeStruct((m, d), F32),
        scratch_shapes=[pltpu.VMEM((tm, d), BF16)],
        compiler_params=_cparams(("parallel", "arbitrary"), 58),
        name="mlp_ln",
    )(x, w_up, w_down, ln_g, ln_b)


def _rope_tables(pos, hd, rot):
    half = rot // 2
    inv_freq = ROPE_THETA ** (-jnp.arange(half, dtype=F32) * 2.0 / rot)
    ang = pos.astype(F32)[:, None] * inv_freq[None, :]
    cos, sin = jnp.cos(ang), jnp.sin(ang)
    lane = jnp.arange(LANE)
    within = lane % hd
    idx = within % half
    cos_l, sin_l = cos[:, idx], sin[:, idx]
    c = jnp.where(within < rot, cos_l, 1.0)
    sa = jnp.where(within < half, -sin_l, 0.0)
    sb = jnp.where((within >= half) & (within < rot), sin_l, 0.0)
    return c, sa, sb


def _lane_vec(v):
    return jnp.broadcast_to(v.astype(F32)[:, None, None], (v.shape[0], 1, LANE))


def kernel(x_prompt, x_sample, cache_k, cache_v, state_ssm, state_conv, page_table, w_in, conv_w, a_log, dt_bias, delta_norm_w, lambda_q1, lambda_k1, lambda_q2, lambda_k2, attn_norm_w, w_branch_a, w_branch_b, w_out, ln1_g, ln1_b, w_mlp_up, w_mlp_down, ln2_g, ln2_b):
    depth, d_model, in_cols = w_in.shape
    bp, sp, _ = x_prompt.shape
    bs, ss, _ = x_sample.shape
    _, n_pool, page, ha, _, hd = cache_k.shape
    _, _, hdn, dk, dv = state_ssm.shape
    dvh = cache_v.shape[-1]
    cw_len = conv_w.shape[1]
    n_pages = page_table.shape[1]
    past_len = n_pages * page
    qk_w = ha * 2 * hd
    wa = ha * dvh
    conv_ch = conv_w.shape[2]
    wdv = hdn * dv
    rot = hd // 4
    off_k, off_v, off_conv = qk_w, 2 * qk_w, 2 * qk_w + wa
    off_z = off_conv + conv_ch
    n_main = off_z + wdv
    off_gate = n_main + 2 * hdn
    assert in_cols == off_gate + 2 * d_model and 2 * hd == LANE and dvh == LANE
    assert qk_w == wa == wdv and conv_ch == 3 * wdv and cw_len <= SUBLANE
    alpha = (2 * depth) ** 0.25
    rows_s = SUBLANE

    w_t = jnp.swapaxes(w_in, 1, 2)
    cache_kt = jnp.transpose(cache_k, (0, 1, 3, 4, 5, 2)).reshape(depth, n_pool, qk_w, page)
    cache_v2 = cache_v.reshape(depth, n_pool, page * ha, dvh)

    alog = jnp.pad(a_log.astype(F32), ((0, 0), (hdn, LANE - 2 * hdn)))[:, None, :]
    dtb = jnp.pad(dt_bias.astype(F32), ((0, 0), (hdn, LANE - 2 * hdn)))[:, None, :]
    cw8 = jnp.pad(conv_w.astype(F32), ((0, 0), (0, SUBLANE - cw_len), (0, 0)))
    dn = delta_norm_w.astype(F32)[:, None, :]
    nw = attn_norm_w.astype(F32)[:, None, :]
    vec3 = lambda a: a.astype(F32)[:, None, :]
    g1, b1, g2, b2 = vec3(ln1_g), vec3(ln1_b), vec3(ln2_g), vec3(ln2_b)
    lam_init = jnp.asarray([0.8 - 0.6 * math.exp(-0.3 * l) for l in range(depth)], F32)
    lam = (jnp.exp(jnp.sum(lambda_q1.astype(F32) * lambda_k1.astype(F32), axis=-1))
           - jnp.exp(jnp.sum(lambda_q2.astype(F32) * lambda_k2.astype(F32), axis=-1)) + lam_init)
    lamv = _lane_vec(lam)

    tabs_p = _rope_tables(jnp.tile(jnp.arange(sp), bp), hd, rot)
    tabs_s = _rope_tables(jnp.tile(past_len + jnp.arange(rows_s), bs), hd, rot)

    mp = bp * sp
    ms = bs * rows_s
    xs_pad = jnp.pad(x_sample, ((0, 0), (0, rows_s - ss), (0, 0))).reshape(ms, d_model)
    xp = x_prompt.reshape(mp, d_model)
    state = {"p": xp, "s": xs_pad}
    buf_p = jnp.zeros((bp, SUBLANE, conv_ch), F32)
    s0_p = jnp.zeros((bp, hdn, dk, dv), F32)

    tn_main = _pick(math.gcd(qk_w, n_main), (1024, 512, 256, 128))
    tn_d = _pick(math.gcd(2 * d_model, n_main), (1024, 512, 256, 128))
    tn_mix = _pick(d_model, (256, 128))
    tf = _pick(w_mlp_up.shape[2], (512, 256, 128))
    tq = _pick(sp, (1024, 512, 256, 128))
    tk_attn = min(tq, 512)
    pp = _pick(n_pages, (16, 8, 4, 2, 1))
    tm_of = {"p": _pick(mp, (1024, 512, 256, 128)), "s": ms}

    outs = {"p": ([], [], [], []), "s": ([], [], [], [])}
    for l in range(depth):
        post = 1.0 - (0.8 - 0.6 * math.exp(-0.3 * l))
        for grp in ("p", "s"):
            xb = state[grp]
            tm = tm_of[grp]
            tabs = tabs_p if grp == "p" else tabs_s
            hmain = _inproj(xb, w_t, l, tabs, n_main=n_main, n_rope=off_v, half=rot // 2,
                            tm=tm, tn=tn_main)
            sg, bg = _gates(xb, w_t, l, alog, dtb, off_small=n_main, n_heads=hdn, n_out=2 * d_model,
                            tm=tm, tn=tn_d)
            if grp == "p":
                oa = _attn_prompt(hmain, lamv, nw, l, batch=bp, seq=sp, n_heads=ha, hd=hd,
                                  post_scale=post, tq=tq, tk=tk_attn, out_dtype=BF16)
                od, ssm = _gdn(hmain, bg, buf_p, s0_p, cw8, dn, l, batch=bp, rows_per_seq=sp,
                               n_valid=sp, n_heads=hdn, dk=dk, dv=dv, conv_w=cw_len,
                               off_conv=off_conv, off_z=off_z, out_dtype=BF16)
                h3 = hmain.reshape(bp, sp, n_main)
                k_new = h3[:, :, off_k:off_v]
                v_new = h3[:, :, off_v:off_conv]
                conv_new = h3[:, sp - (cw_len - 1):, off_conv:off_z]
            else:
                oa = _attn_decode(hmain, cache_kt, cache_v2, page_table, lamv, nw, l, n_heads=ha,
                                  hd=hd, n_tok=ss, post_scale=post, pp=pp)
                buf_s = jnp.pad(state_conv[l].astype(F32),
                                ((0, 0), (SUBLANE - (cw_len - 1), 0), (0, 0)))
                od, ssm = _gdn(hmain, bg, buf_s, state_ssm[l].astype(F32), cw8, dn, l, batch=bs,
                               rows_per_seq=rows_s, n_valid=ss, n_heads=hdn, dk=dk, dv=dv,
                               conv_w=cw_len, off_conv=off_conv, off_z=off_z, out_dtype=F32)
                h3 = hmain.reshape(bs, rows_s, n_main)
                k_new = h3[:, :ss, off_k:off_v]
                v_new = h3[:, :ss, off_v:off_conv]
                conv_new = h3[:, ss - (cw_len - 1):ss, off_conv:off_z]
            x1 = _merge_proj_ln(oa, od, w_branch_a, w_branch_b, sg, w_out, xb, g1, b1, l, alpha=alpha,
                                tm=tm, tn=tn_mix)
            state[grp] = _mlp_ln(x1, w_mlp_up, w_mlp_down, g2, b2, l, alpha=alpha, tm=tm, tf=tf)
            ko, vo, so, co = outs[grp]
            nb = k_new.shape[0]
            ko.append(k_new.reshape(nb, -1, ha, 2, hd))
            vo.append(v_new.reshape(nb, -1, ha, dvh))
            so.append(ssm)
            co.append(conv_new)

    y_p = state["p"].reshape(bp, sp, d_model)
    y_s = state["s"].reshape(bs, rows_s, d_model)[:, :ss]
    kp, vp, sp_l, cp = (jnp.stack(t) for t in outs["p"])
    ks, vs, ss_l, cs = (jnp.stack(t) for t in outs["s"])
    return (y_p, y_s, kp, vp, sp_l, cp, ks, vs, ss_l, cs)
```
